```python
import math
import jax, jax.numpy as jnp
from jax import lax
import numpy as np

D_MODEL = 1024
BATCH = 16
SEQ = 4096
DEPTH = 2
DEC_BATCH = 2
DEC_SEQ = 16384
PAST_LEN = 128

HEAD_DIM = 64
EPS = 1e-6
H_A = 4
W_A = H_A * 2 * HEAD_DIM
Q_BLOCK = 128
H_B = 8
W_B = H_B * HEAD_DIM
GRID_W = 64
MAX_KH = 8
KW = 16
H_C = 8
KV_C = 2
W_C = H_C * HEAD_DIM
KV_W_C = KV_C * HEAD_DIM
WINDOW = 128
BLOCK = 128

_SIZES = (W_A, W_A, W_A, W_A,
          W_B, W_B, W_B, W_B,
          W_C, KV_W_C, KV_W_C, W_C,
          D_MODEL, D_MODEL, D_MODEL)
IN_COLS = sum(_SIZES)
SPLIT_POINTS = tuple(sum(_SIZES[:i + 1]) for i in range(len(_SIZES) - 1))

kernel_name = "hybrid_diff_na2d_swa_encoder"


def rms_norm(x, g):
    xf = x.astype(jnp.float32)
    y = xf * lax.rsqrt(jnp.mean(xf * xf, axis=-1, keepdims=True) + EPS)
    return (y * g.astype(jnp.float32)).astype(x.dtype)


def alibi_slopes(n):
    return jnp.asarray(np.array([2.0 ** (-8.0 * (i + 1) / n) for i in range(n)], dtype=np.float32))


def diff_attention(q, k, v, lam, lam_init, subln_g):
    B, T = q.shape[0], q.shape[1]
    nq = T // Q_BLOCK
    slopes = alibi_slopes(H_A)[None, :, None, None, None]
    kpos = jnp.arange(T)
    scale = HEAD_DIM ** -0.5
    qb = q.reshape(B, nq, Q_BLOCK, H_A, 2, HEAD_DIM).transpose(1, 0, 2, 3, 4, 5)

    def block(args):
        qi, i = args
        s = jnp.einsum('bqhcd,bkhcd->bhcqk', qi, k).astype(jnp.float32) * scale
        qpos = i * Q_BLOCK + jnp.arange(Q_BLOCK)
        dist = jnp.abs(qpos[:, None] - kpos[None, :]).astype(jnp.float32)
        p = jax.nn.softmax(s - slopes * dist, axis=-1)
        a = p[:, :, 0] - lam * p[:, :, 1]
        return jnp.einsum('bhqk,bkhe->bqhe', a.astype(v.dtype), v)

    o = lax.map(block, (qb, jnp.arange(nq)))
    o = o.transpose(1, 0, 2, 3, 4).reshape(B, T, H_A, 2 * HEAD_DIM)
    o = rms_norm(o, subln_g) * (1.0 - lam_init)
    return o.reshape(B, T, W_A)


def neighbourhood_attention(q, k, v, rpb):
    B, T = q.shape[0], q.shape[1]
    rows = T // GRID_W
    kh = min(MAX_KH, rows)
    qg = q.reshape(B, rows, GRID_W, H_B, HEAD_DIM)
    kg = k.reshape(B, rows, GRID_W, H_B, HEAD_DIM)
    vg = v.reshape(B, rows, GRID_W, H_B, HEAD_DIM)
    cols = jnp.arange(GRID_W)
    cstart = jnp.clip(cols - KW // 2, 0, GRID_W - KW)
    cidx = cstart[:, None] + jnp.arange(KW)[None, :]
    dc = cidx - cols[:, None] + (KW - 1)
    scale = HEAD_DIM ** -0.5

    def row(r):
        rs = jnp.clip(r - kh // 2, 0, rows - kh)
        q_r = lax.dynamic_index_in_dim(qg, r, axis=1, keepdims=False)
        k_rows = lax.dynamic_slice_in_dim(kg, rs, kh, axis=1)
        v_rows = lax.dynamic_slice_in_dim(vg, rs, kh, axis=1)
        k_n = k_rows[:, :, cidx]
        v_n = v_rows[:, :, cidx]
        dr = rs + jnp.arange(kh) - r + (MAX_KH - 1)
        bias = rpb[:, dr[None, :, None], dc[:, None, :]].astype(jnp.float32)
        s = jnp.einsum('bchd,bicjhd->bhcij', q_r, k_n).astype(jnp.float32) * scale + bias[None]
        p = jax.nn.softmax(s.reshape(B, H_B, GRID_W, kh * KW), axis=-1)
        p = p.reshape(B, H_B, GRID_W, kh, KW).astype(v.dtype)
        return jnp.einsum('bhcij,bicjhd->bchd', p, v_n)

    o = lax.map(row, jnp.arange(rows))
    return o.transpose(1, 0, 2, 3, 4).reshape(B, T, W_B)


def sliding_window_attention(q, k, v, sink):
    B, T = q.shape[0], q.shape[1]
    n = T // BLOCK
    G = H_C // KV_C
    scale = HEAD_DIM ** -0.5

    def band(a):
        ap = jnp.pad(a, ((0, 0), (BLOCK, BLOCK), (0, 0), (0, 0))).reshape(B, n + 2, BLOCK, KV_C, HEAD_DIM)
        return jnp.concatenate([ap[:, :-2], ap[:, 1:-1], ap[:, 2:]], axis=2)

    kb, vb = band(k), band(v)
    qb = q.reshape(B, n, BLOCK, KV_C, G, HEAD_DIM)
    s = jnp.einsum('bnqkgd,bnskd->bnkgqs', qb, kb).astype(jnp.float32) * scale
    qpos = jnp.arange(n)[:, None] * BLOCK + jnp.arange(BLOCK)[None, :]
    kpos = jnp.arange(n)[:, None] * BLOCK - BLOCK + jnp.arange(3 * BLOCK)[None, :]
    dist_i = jnp.abs(qpos[:, :, None] - kpos[:, None, :])
    valid = (dist_i <= WINDOW) & (kpos >= 0)[:, None, :] & (kpos < T)[:, None, :]
    dist = dist_i.astype(jnp.float32)[:, None, None]
    slopes = alibi_slopes(H_C).reshape(KV_C, G)[:, :, None, None]
    s = jnp.where(valid[:, None, None], s - slopes * dist, -1e30)
    sink_col = jnp.broadcast_to(sink.astype(jnp.float32).reshape(KV_C, G)[None, None, :, :, None, None],
                                s.shape[:-1] + (1,))
    p = jax.nn.softmax(jnp.concatenate([s, sink_col], axis=-1), axis=-1)[..., :-1]
    o = jnp.einsum('bnkgqs,bnskd->bnqkgd', p.astype(v.dtype), vb)
    return o.reshape(B, T, W_C)


def trunk(x, norm_g, w_in, qk_gain_a, lambda_a, subln_g_a, qk_gain_b, rpb_b,
          qk_gain_c, sink_c, w_proj_a, w_proj_b, w_proj_c, w_out):
    B, T = x.shape[0], x.shape[1]
    for l in range(DEPTH):
        h = rms_norm(x, norm_g[l])
        p = jnp.einsum('btd,dc->btc', h, w_in[l])
        (qa, ka, va, za, qb, kb, vb, zb, qc, kc, vc, zc,
         ga, gb, gc) = jnp.split(p, SPLIT_POINTS, axis=-1)
        lam_init = 0.8 - 0.6 * math.exp(-0.3 * l)
        lam_vec = lambda_a[l].astype(jnp.float32)
        lam = (jnp.exp(jnp.sum(lam_vec[0] * lam_vec[1])) - jnp.exp(jnp.sum(lam_vec[2] * lam_vec[3]))
               + lam_init)
        qa = rms_norm(qa.reshape(B, T, H_A, 2, HEAD_DIM), qk_gain_a[l, 0])
        ka = rms_norm(ka.reshape(B, T, H_A, 2, HEAD_DIM), qk_gain_a[l, 1])
        oa = diff_attention(qa, ka, va.reshape(B, T, H_A, 2 * HEAD_DIM), lam, lam_init, subln_g_a[l])
        qb = rms_norm(qb.reshape(B, T, H_B, HEAD_DIM), qk_gain_b[l, 0])
        kb = rms_norm(kb.reshape(B, T, H_B, HEAD_DIM), qk_gain_b[l, 1])
        ob = neighbourhood_attention(qb, kb, vb.reshape(B, T, H_B, HEAD_DIM), rpb_b[l])
        qc = rms_norm(qc.reshape(B, T, H_C, HEAD_DIM), qk_gain_c[l, 0])
        kc = rms_norm(kc.reshape(B, T, KV_C, HEAD_DIM), qk_gain_c[l, 1])
        oc = sliding_window_attention(qc, kc, vc.reshape(B, T, KV_C, HEAD_DIM), sink_c[l])
        ya = jnp.einsum('btw,wd->btd', oa * jax.nn.silu(za), w_proj_a[l])
        yb = jnp.einsum('btw,wd->btd', ob * jax.nn.silu(zb), w_proj_b[l])
        yc = jnp.einsum('btw,wd->btd', oc * jax.nn.silu(zc), w_proj_c[l])
        m = jax.nn.sigmoid(ga) * ya + jax.nn.sigmoid(gb) * yb + jax.nn.sigmoid(gc) * yc
        x = x + jnp.einsum('btd,de->bte', m, w_out[l])
    return x


def setup_inputs(seed: int = 0) -> dict:
    key = jax.random.key(seed)
    ks = jax.random.split(key, 16)
    f32 = jnp.float32
    nrm = lambda k, shape, s: (jax.random.normal(k, shape, f32) * s).astype(f32)
    return {
        "x_prompt": nrm(ks[0], (BATCH, SEQ, D_MODEL), 1.0),
        "x_sample": nrm(ks[1], (DEC_BATCH, DEC_SEQ, D_MODEL), 1.0),
        "norm_g": 1.0 + nrm(ks[2], (DEPTH, D_MODEL), 0.1),
        "w_in": nrm(ks[3], (DEPTH, D_MODEL, IN_COLS), D_MODEL ** -0.5),
        "qk_gain_a": 1.0 + nrm(ks[4], (DEPTH, 2, HEAD_DIM), 0.1),
        "lambda_a": nrm(ks[5], (DEPTH, 4, HEAD_DIM), 0.1),
        "subln_g_a": 1.0 + nrm(ks[6], (DEPTH, 2 * HEAD_DIM), 0.1),
        "qk_gain_b": 1.0 + nrm(ks[7], (DEPTH, 2, HEAD_DIM), 0.1),
        "rpb_b": nrm(ks[8], (DEPTH, H_B, 2 * MAX_KH - 1, 2 * KW - 1), 0.5),
        "qk_gain_c": 1.0 + nrm(ks[9], (DEPTH, 2, HEAD_DIM), 0.1),
        "sink_c": nrm(ks[10], (DEPTH, H_C), 1.0),
        "w_proj_a": nrm(ks[11], (DEPTH, W_A, D_MODEL), W_A ** -0.5),
        "w_proj_b": nrm(ks[12], (DEPTH, W_B, D_MODEL), W_B ** -0.5),
        "w_proj_c": nrm(ks[13], (DEPTH, W_C, D_MODEL), W_C ** -0.5),
        "w_out": nrm(ks[14], (DEPTH, D_MODEL, D_MODEL), D_MODEL ** -0.5),
    }


def reference(x_prompt, x_sample, norm_g, w_in, qk_gain_a, lambda_a, subln_g_a, qk_gain_b, rpb_b,
              qk_gain_c, sink_c, w_proj_a, w_proj_b, w_proj_c, w_out):
    y_prompt = trunk(x_prompt, norm_g, w_in, qk_gain_a, lambda_a, subln_g_a, qk_gain_b, rpb_b,
                     qk_gain_c, sink_c, w_proj_a, w_proj_b, w_proj_c, w_out)
    y_sample = trunk(x_sample, norm_g, w_in, qk_gain_a, lambda_a, subln_g_a, qk_gain_b, rpb_b,
                     qk_gain_c, sink_c, w_proj_a, w_proj_b, w_proj_c, w_out)
    return (y_prompt, y_sample)
```

```python
import functools
import math

import numpy as np
import jax
import jax.numpy as jnp
from jax import lax
from jax.experimental import pallas as pl
from jax.experimental.pallas import tpu as pltpu

D_MODEL = 1024
DEPTH = 2
HEAD_DIM = 64
EPS = 1e-6
H_A = 4
W_A = H_A * 2 * HEAD_DIM
H_B = 8
W_B = H_B * HEAD_DIM
GRID_W = 64
MAX_KH = 8
KW = 16
H_C = 8
KV_C = 2
W_C = H_C * HEAD_DIM
WINDOW = 128

LOG2E = math.log2(math.e)
QK_SCALE = HEAD_DIM ** -0.5
MASKED = -1e30

LANES = 128
TM = 512
TQ_A = 256
TK_A = TM
TQ_W = 256
WIN_B = (MAX_KH + 4) * GRID_W
WIN_C = TQ_W + 2 * WINDOW
N_BIAS_LANES = 3
VMEM_LIMIT_BYTES = 56 * 1024 * 1024

_TOK_SIZES = (("ka", W_A), ("za", W_A), ("qb", W_B), ("kb", W_B), ("vb", W_B), ("zb", W_B),
              ("qc", W_C), ("kc", 4 * HEAD_DIM), ("vc", 4 * HEAD_DIM), ("zc", W_C),
              ("ga", D_MODEL), ("gb", D_MODEL), ("gc", D_MODEL))
_TOK_OFF = {}
_o = 0
for _n, _s in _TOK_SIZES:
    _TOK_OFF[_n] = (_o, _o + _s)
    _o += _s
TOK_COLS = _o


def _alibi_slopes(n):
    return [2.0 ** (-8.0 * (i + 1) / n) for i in range(n)]


def _params(**kw):
    return pltpu.CompilerParams(vmem_limit_bytes=VMEM_LIMIT_BYTES, **kw)


def _resident(block_shape, index_map):
    return pl.BlockSpec(block_shape, index_map, pipeline_mode=pl.Buffered(1))


def _inproj_kernel(x_ref, ng_ref, wtok_ref, wfeat_ref, gmat_ref, gains_ref, gfeat_ref,
                   qat_ref, ka_ref, vat_ref, za_ref, qb_ref, kb_ref, vb_ref, zb_ref,
                   qc_ref, kc_ref, vc_ref, zc_ref, g_ref):
    bf16, f32 = jnp.bfloat16, jnp.float32
    x = x_ref[0]
    ms = jnp.mean(x * x, axis=-1, keepdims=True)
    h = (x * lax.rsqrt(ms + EPS) * ng_ref[...]).astype(bf16)

    def proj(name):
        c0, c1 = _TOK_OFF[name]
        return jnp.dot(h, wtok_ref[:, c0:c1], preferred_element_type=f32)

    def normed(name, gain_row):
        p = proj(name)
        w = p.shape[-1]
        ss = jnp.dot((p * p).astype(bf16), gmat_ref[:w, :w], preferred_element_type=f32)
        return (p * lax.rsqrt(ss + EPS) * gains_ref[gain_row:gain_row + 1, :w]).astype(bf16)

    ka_ref[0] = normed("ka", 0)
    za_ref[0] = proj("za").astype(bf16)
    qb_ref[0] = normed("qb", 1)
    kb_ref[0] = normed("kb", 2)
    vb_ref[0] = proj("vb").astype(bf16)
    zb_ref[0] = proj("zb").astype(bf16)
    qc_ref[0] = normed("qc", 3)
    kc_ref[0] = normed("kc", 4)
    vc_ref[0] = proj("vc").astype(bf16)
    zc_ref[0] = proj("zc").astype(bf16)
    for i, name in enumerate(("ga", "gb", "gc")):
        g_ref[0, :, i * D_MODEL:(i + 1) * D_MODEL] = proj(name).astype(bf16)

    pf = lax.dot_general(wfeat_ref[...], h, (((1,), (1,)), ((), ())), preferred_element_type=f32)
    for g in range(W_A // HEAD_DIM):
        blk = pf[g * HEAD_DIM:(g + 1) * HEAD_DIM]
        r = lax.rsqrt(jnp.mean(blk * blk, axis=0, keepdims=True) + EPS)
        qat_ref[0, g * HEAD_DIM:(g + 1) * HEAD_DIM, :] = (blk * r * gfeat_ref[...]).astype(bf16)
    for hh in range(H_A):
        lo = W_A + hh * 2 * HEAD_DIM
        vat_ref[0, hh, 0] = pf[lo:lo + 2 * HEAD_DIM].astype(bf16)


def _inproj(x, ng, wtok, wfeat, gmat, gains, gfeat):
    B, T, _ = x.shape
    nt = T // TM
    bf16 = jnp.bfloat16
    tok = lambda w: pl.BlockSpec((1, TM, w), lambda b, t: (b, t, 0))
    const2 = lambda a: _resident(a.shape, lambda b, t: (0, 0))
    out_shape = (
        jax.ShapeDtypeStruct((B, W_A, T), bf16),
        jax.ShapeDtypeStruct((B, T, W_A), bf16),
        jax.ShapeDtypeStruct((B, H_A, nt, 2 * HEAD_DIM, TM), bf16),
        jax.ShapeDtypeStruct((B, T, W_A), bf16),
        jax.ShapeDtypeStruct((B, T, W_B), bf16),
        jax.ShapeDtypeStruct((B, T, W_B), bf16),
        jax.ShapeDtypeStruct((B, T, W_B), bf16),
        jax.ShapeDtypeStruct((B, T, W_B), bf16),
        jax.ShapeDtypeStruct((B, T, W_C), bf16),
        jax.ShapeDtypeStruct((B, T, 4 * HEAD_DIM), bf16),
        jax.ShapeDtypeStruct((B, T, 4 * HEAD_DIM), bf16),
        jax.ShapeDtypeStruct((B, T, W_C), bf16),
        jax.ShapeDtypeStruct((B, T, 3 * D_MODEL), bf16),
    )
    out_specs = (
        pl.BlockSpec((1, W_A, TM), lambda b, t: (b, 0, t)),
        tok(W_A),
        pl.BlockSpec((1, H_A, 1, 2 * HEAD_DIM, TM), lambda b, t: (b, 0, t, 0, 0)),
        tok(W_A), tok(W_B), tok(W_B), tok(W_B), tok(W_B), tok(W_C),
        tok(4 * HEAD_DIM), tok(4 * HEAD_DIM), tok(W_C), tok(3 * D_MODEL),
    )
    return pl.pallas_call(
        _inproj_kernel,
        grid=(B, nt),
        in_specs=[pl.BlockSpec((1, TM, D_MODEL), lambda b, t: (b, t, 0)),
                  const2(ng), const2(wtok), const2(wfeat), const2(gmat), const2(gains), const2(gfeat)],
        out_specs=out_specs,
        out_shape=out_shape,
        compiler_params=_params(dimension_semantics=("parallel", "parallel")),
        name="inproj",
    )(x, ng, wtok, wfeat, gmat, gains, gfeat)


def _diff_attn_kernel(c_ref, lam_ref, qt_ref, k_ref, vt_ref, kbias_ref, g2_ref, o_ref,
                      kp_ref, qv_ref, m_ref, l_ref, acc_ref, *, seq):
    bf16, f32 = jnp.bfloat16, jnp.float32
    h = pl.program_id(1)
    qi = pl.program_id(2)
    nk = seq // TK_A
    c = c_ref[h]

    @pl.when(qi == 0)
    def _():
        lane = lax.broadcasted_iota(jnp.int32, (TK_A, LANES), 1)
        kb0 = kbias_ref[0, 0].astype(f32)
        kb1 = kbias_ref[0, 1].astype(f32)

        def body(j, carry):
            rows = pl.ds(pl.multiple_of(j * TK_A, TK_A), TK_A)
            kblk = k_ref[0, rows, :].astype(f32)
            kp_ref[0, rows, :] = jnp.where(lane < HEAD_DIM, kblk, kb0).astype(bf16)
            kp_ref[1, rows, :] = jnp.where(lane >= HEAD_DIM, kblk, kb1).astype(bf16)
            return carry

        lax.fori_loop(0, nk, body, 0)

    row = lax.broadcasted_iota(jnp.int32, (2 * HEAD_DIM, TQ_A), 0)
    qblk = qt_ref[0].astype(f32)
    for sub in range(2):
        own = (row < HEAD_DIM) if sub == 0 else (row >= HEAD_DIM)
        b0 = HEAD_DIM if sub == 0 else 0
        brow = (row >= b0) & (row < b0 + N_BIAS_LANES)
        base = jnp.where(own, qblk, 0.0)
        for var, sigma in enumerate((1.0, -1.0, 0.0)):
            qv_ref[3 * sub + var] = jnp.where(brow, sigma, base).astype(bf16)

    m_ref[...] = jnp.full(m_ref.shape, MASKED, f32)
    l_ref[...] = jnp.zeros(l_ref.shape, f32)
    acc_ref[...] = jnp.zeros(acc_ref.shape, f32)

    i0 = qi * TQ_A
    jd = i0 // TK_A
    di = lax.broadcasted_iota(jnp.int32, (1, TQ_A), 1)

    def update(sub, t, rvec, vt):
        tmax = jnp.max(t, axis=0, keepdims=True)
        m_old = m_ref[sub]
        m_new = jnp.maximum(m_old, tmax + rvec)
        alpha = jnp.exp2(m_old - m_new)
        p = jnp.exp2(t - (m_new - rvec))
        l_ref[sub] = alpha * l_ref[sub] + jnp.sum(p, axis=0, keepdims=True)
        acc_ref[sub] = alpha * acc_ref[sub] + jnp.dot(vt, p.astype(bf16), preferred_element_type=f32)
        m_ref[sub] = m_new

    rows_d = pl.ds(pl.multiple_of(jd * TK_A, TK_A), TK_A)
    off = i0 - jd * TK_A
    dj = lax.broadcasted_iota(jnp.int32, (TK_A, TQ_A), 0)
    dq = lax.broadcasted_iota(jnp.int32, (TK_A, TQ_A), 1)
    dbias = -c * jnp.abs(off + dq - dj).astype(f32)
    vt_d = vt_ref[0, 0, jd]
    zero_r = jnp.zeros((1, TQ_A), f32)
    for sub in range(2):
        t = jnp.dot(kp_ref[sub, rows_d, :], qv_ref[3 * sub + 2], preferred_element_type=f32) + dbias
        update(sub, t, zero_r, vt_d)

    def body(jj, carry):
        j = jj + (jj >= jd).astype(jnp.int32)
        after = j > jd
        var = after.astype(jnp.int32)
        sgn = jnp.where(after, 1.0, -1.0).astype(f32)
        rvec = (sgn * c) * (i0 - j * TK_A + di).astype(f32)
        rows = pl.ds(pl.multiple_of(j * TK_A, TK_A), TK_A)
        vt = vt_ref[0, 0, j]
        for sub in range(2):
            t = jnp.dot(kp_ref[sub, rows, :], qv_ref[3 * sub + var], preferred_element_type=f32)
            update(sub, t, rvec, vt)
        return carry

    lax.fori_loop(0, nk - 1, body, 0)

    o = acc_ref[0] / l_ref[0] - lam_ref[0] * (acc_ref[1] / l_ref[1])
    y = o * lax.rsqrt(jnp.mean(o * o, axis=0, keepdims=True) + EPS) * g2_ref[...]
    o_ref[0] = y.T.astype(o_ref.dtype)


def _diff_attn(cvec, lam, qat, ka, vat, kbias, g2):
    B, _, T = qat.shape
    nq = T // TQ_A
    nkb = T // TK_A
    smem = pl.BlockSpec(memory_space=pltpu.SMEM)
    return pl.pallas_call(
        functools.partial(_diff_attn_kernel, seq=T),
        grid=(B, H_A, nq),
        in_specs=[
            smem, smem,
            pl.BlockSpec((1, 2 * HEAD_DIM, TQ_A), lambda b, h, q: (b, h, q)),
            pl.BlockSpec((1, T, 2 * HEAD_DIM), lambda b, h, q: (b, 0, h)),
            pl.BlockSpec((1, 1, nkb, 2 * HEAD_DIM, TK_A), lambda b, h, q: (b, h, 0, 0, 0)),
            pl.BlockSpec((1, 2, TK_A, LANES), lambda b, h, q: (h, 0, 0, 0)),
            pl.BlockSpec((2 * HEAD_DIM, 1), lambda b, h, q: (0, 0)),
        ],
        out_specs=pl.BlockSpec((1, TQ_A, 2 * HEAD_DIM), lambda b, h, q: (b, q, h)),
        out_shape=jax.ShapeDtypeStruct((B, T, W_A), jnp.bfloat16),
        scratch_shapes=[
            pltpu.VMEM((2, T, LANES), jnp.bfloat16),
            pltpu.VMEM((6, 2 * HEAD_DIM, TQ_A), jnp.bfloat16),
            pltpu.VMEM((2, 1, TQ_A), jnp.float32),
            pltpu.VMEM((2, 1, TQ_A), jnp.float32),
            pltpu.VMEM((2, 2 * HEAD_DIM, TQ_A), jnp.float32),
        ],
        compiler_params=_params(dimension_semantics=("parallel", "parallel", "arbitrary")),
        name="diff_attn",
    )(cvec, lam, qat, ka, vat, kbias, g2)


def _win_attn_kernel(sink_ref, q_ref, k_ref, v_ref, bias_ref, o_ref, *, seq, win, lead):
    bf16, f32 = jnp.bfloat16, jnp.float32
    pair = pl.program_id(1)
    qi = pl.program_id(2)
    ws = jnp.clip(qi * TQ_W - lead, 0, seq - win)
    rows = pl.ds(pl.multiple_of(ws, GRID_W), win)
    kw = k_ref[0, rows, :]
    vw = v_ref[0, rows, :]
    q = q_ref[0].astype(f32)
    lane = lax.broadcasted_iota(jnp.int32, (TQ_W, LANES), 1)
    outs = []
    for half in range(2):
        own = (lane < HEAD_DIM) if half == 0 else (lane >= HEAD_DIM)
        qm = jnp.where(own, q, 0.0).astype(bf16)
        s = lax.dot_general(qm, kw, (((1,), (1,)), ((), ())), preferred_element_type=f32)
        s = s + bias_ref[0, half]
        sink = sink_ref[2 * pair + half]
        m = jnp.maximum(jnp.max(s, axis=-1, keepdims=True), sink)
        p = jnp.exp2(s - m)
        l = jnp.sum(p, axis=-1, keepdims=True) + jnp.exp2(sink - m)
        outs.append(jnp.dot(p.astype(bf16), vw, preferred_element_type=f32) / l)
    o_ref[0] = jnp.where(lane < HEAD_DIM, outs[0], outs[1]).astype(o_ref.dtype)


def _win_attn(sink2, q, k, v, bias, *, win, lead, kv_group):
    B, T, _ = q.shape
    nq = T // TQ_W
    n_pairs = q.shape[-1] // LANES
    variant = lambda qi: jnp.where(qi == 0, 0, jnp.where(qi == nq - 1, 2, 1))
    return pl.pallas_call(
        functools.partial(_win_attn_kernel, seq=T, win=win, lead=lead),
        grid=(B, n_pairs, nq),
        in_specs=[
            pl.BlockSpec(memory_space=pltpu.SMEM),
            pl.BlockSpec((1, TQ_W, LANES), lambda b, p, qi: (b, qi, p)),
            pl.BlockSpec((1, T, LANES), lambda b, p, qi: (b, 0, p // kv_group)),
            pl.BlockSpec((1, T, LANES), lambda b, p, qi: (b, 0, p // kv_group)),
            pl.BlockSpec((1, 2, TQ_W, win), lambda b, p, qi: (variant(qi), p, 0, 0)),
        ],
        out_specs=pl.BlockSpec((1, TQ_W, LANES), lambda b, p, qi: (b, qi, p)),
        out_shape=jax.ShapeDtypeStruct(q.shape, jnp.bfloat16),
        compiler_params=_params(dimension_semantics=("parallel", "parallel", "arbitrary")),
        name="win_attn",
    )(sink2, q, k, v, bias)


def _merge_kernel(x_ref, oa_ref, ob_ref, oc_ref, za_ref, zb_ref, zc_ref, g_ref,
                  wa_ref, wb_ref, wc_ref, wo_ref, y_ref):
    bf16, f32 = jnp.bfloat16, jnp.float32

    def branch(o_ref, z_ref, w_ref, i):
        z = z_ref[0].astype(f32)
        u = o_ref[0].astype(f32) * (z * jax.nn.sigmoid(z))
        y = jnp.dot(u.astype(bf16), w_ref[...], preferred_element_type=f32)
        gate = g_ref[0, :, i * D_MODEL:(i + 1) * D_MODEL].astype(f32)
        return jax.nn.sigmoid(gate) * y

    m = branch(oa_ref, za_ref, wa_ref, 0) + branch(ob_ref, zb_ref, wb_ref, 1) + branch(oc_ref, zc_ref, wc_ref, 2)
    y_ref[0] = x_ref[0] + jnp.dot(m.astype(bf16), wo_ref[...], preferred_element_type=f32)


def _merge(x, oa, ob, oc, za, zb, zc, g, wa, wb, wc, wo):
    B, T, _ = x.shape
    tok = lambda w: pl.BlockSpec((1, TM, w), lambda b, t: (b, t, 0))
    const2 = lambda a: _resident(a.shape, lambda b, t: (0, 0))
    return pl.pallas_call(
        _merge_kernel,
        grid=(B, T // TM),
        in_specs=[tok(D_MODEL), tok(W_A), tok(W_B), tok(W_C), tok(W_A), tok(W_B), tok(W_C), tok(3 * D_MODEL),
                  const2(wa), const2(wb), const2(wc), const2(wo)],
        out_specs=tok(D_MODEL),
        out_shape=jax.ShapeDtypeStruct(x.shape, x.dtype),
        compiler_params=_params(dimension_semantics=("parallel", "parallel")),
        name="merge",
    )(x, oa, ob, oc, za, zb, zc, g, wa, wb, wc, wo)


def _split3_bf16(v):
    hi = v.astype(jnp.bfloat16)
    r1 = v - hi.astype(jnp.float32)
    mid = r1.astype(jnp.bfloat16)
    lo = (r1 - mid.astype(jnp.float32)).astype(jnp.bfloat16)
    return hi, mid, lo


def _diff_constants():
    c = np.asarray(_alibi_slopes(H_A), np.float32) * np.float32(LOG2E)
    col = jnp.asarray(c)[:, None] * jnp.arange(TK_A, dtype=jnp.float32)[None, :]
    pieces = jnp.stack(_split3_bf16(col), axis=-1)
    kbias = jnp.zeros((H_A, 2, TK_A, LANES), jnp.bfloat16)
    kbias = kbias.at[:, 0, :, HEAD_DIM:HEAD_DIM + N_BIAS_LANES].set(pieces)
    kbias = kbias.at[:, 1, :, 0:N_BIAS_LANES].set(pieces)
    return jnp.asarray(c), kbias


def _window_bias_c():
    slopes = np.asarray(_alibi_slopes(H_C), np.float64)
    di = np.arange(TQ_W)[:, None]
    dj = np.arange(WIN_C)[None, :]
    tiles = []
    for lead in (0, WINDOW, WIN_C - TQ_W):
        dist = np.abs(lead + di - dj)
        tile = np.where(dist <= WINDOW, -(slopes[:, None, None] * LOG2E) * dist[None], MASKED)
        tiles.append(tile)
    return jnp.asarray(np.stack(tiles).astype(np.float32))


def _window_bias_b(rpb):
    q_rows = TQ_W // GRID_W
    w_rows = WIN_B // GRID_W
    qr = (np.arange(TQ_W) // GRID_W)[:, None]
    qc = (np.arange(TQ_W) % GRID_W)[:, None]
    wr = (np.arange(WIN_B) // GRID_W)[None, :]
    kc = (np.arange(WIN_B) % GRID_W)[None, :]
    cstart = np.clip(qc - KW // 2, 0, GRID_W - KW)
    col_ok = (kc >= cstart) & (kc < cstart + KW)
    dc = np.clip(kc - qc + (KW - 1), 0, 2 * KW - 2)
    tiles = []
    for lead_rows, rs_of in ((0, lambda r: 0 * r), (MAX_KH // 2, lambda r: r), (w_rows - q_rows, lambda r: 0 * r + MAX_KH // 2)):
        rs = rs_of(qr)
        row_ok = (wr >= rs) & (wr < rs + MAX_KH)
        dr = np.clip(wr - (lead_rows + qr) + (MAX_KH - 1), 0, 2 * MAX_KH - 2)
        dr_b = np.broadcast_to(dr, (TQ_W, WIN_B))
        dc_b = np.broadcast_to(dc, (TQ_W, WIN_B))
        vals = rpb[:, dr_b, dc_b].astype(jnp.float32) * LOG2E
        tiles.append(jnp.where(jnp.asarray(row_ok & col_ok)[None], vals, MASKED))
    return jnp.stack(tiles)


def _layer_weights(l, w_in, norm_g, qk_gain_a, qk_gain_b, qk_gain_c):
    bf16, f32 = jnp.bfloat16, jnp.float32
    w = w_in[l]
    sizes = (W_A,) * 4 + (W_B,) * 4 + (W_C, KV_C * HEAD_DIM, KV_C * HEAD_DIM, W_C) + (D_MODEL,) * 3
    offs = np.concatenate([[0], np.cumsum(sizes)])
    names = ("qa", "ka", "va", "za", "qb", "kb", "vb", "zb", "qc", "kc", "vc", "zc", "ga", "gb", "gc")
    cols = {n: w[:, offs[i]:offs[i + 1]] for i, n in enumerate(names)}
    dup = lambda a: jnp.concatenate([a[:, :HEAD_DIM], a[:, :HEAD_DIM], a[:, HEAD_DIM:], a[:, HEAD_DIM:]], axis=1)
    cols["kc"] = dup(cols["kc"])
    cols["vc"] = dup(cols["vc"])
    wtok = jnp.concatenate([cols[n] for n, _ in _TOK_SIZES], axis=1).astype(bf16)
    wfeat = jnp.concatenate([cols["qa"], cols["va"]], axis=1).T.astype(bf16)
    qscale = QK_SCALE * LOG2E
    tile8 = lambda g: jnp.tile(g.astype(f32), W_A // HEAD_DIM)
    gains = jnp.stack([tile8(qk_gain_a[l, 1]), tile8(qk_gain_b[l, 0]) * qscale, tile8(qk_gain_b[l, 1]),
                       tile8(qk_gain_c[l, 0]) * qscale, tile8(qk_gain_c[l, 1]),
                       jnp.zeros((W_A,), f32), jnp.zeros((W_A,), f32), jnp.zeros((W_A,), f32)])
    gfeat = (qk_gain_a[l, 0].astype(f32) * qscale)[:, None]
    ng = norm_g[l].astype(f32)[None, :]
    return ng, wtok, wfeat, gains, gfeat


def _trunk(x, layers, consts):
    cvec, kbias, gmat, bias_c, no_sink = consts
    for lw in layers:
        (ng, wtok, wfeat, gains, gfeat, lam, g2, bias_b, sink2, wa, wb, wc, wo) = lw
        (qat, ka, vat, za, qb, kb, vb, zb, qc, kc, vc, zc, g) = _inproj(x, ng, wtok, wfeat, gmat, gains, gfeat)
        oa = _diff_attn(cvec, lam, qat, ka, vat, kbias, g2)
        ob = _win_attn(no_sink, qb, kb, vb, bias_b, win=WIN_B, lead=(MAX_KH // 2) * GRID_W, kv_group=1)
        oc = _win_attn(sink2, qc, kc, vc, bias_c, win=WIN_C, lead=WINDOW, kv_group=2)
        x = _merge(x, oa, ob, oc, za, zb, zc, g, wa, wb, wc, wo)
    return x


def kernel(x_prompt, x_sample, norm_g, w_in, qk_gain_a, lambda_a, subln_g_a, qk_gain_b, rpb_b,
           qk_gain_c, sink_c, w_proj_a, w_proj_b, w_proj_c, w_out):
    bf16, f32 = jnp.bfloat16, jnp.float32
    for x in (x_prompt, x_sample):
        T = x.shape[1]
        assert T % TM == 0 and T >= 3 * TQ_W and T >= WIN_B and x.shape[2] == D_MODEL
    cvec, kbias = _diff_constants()
    gidx = np.arange(W_A) // HEAD_DIM
    gmat = jnp.asarray((gidx[:, None] == gidx[None, :]).astype(np.float32) / HEAD_DIM, bf16)
    consts = (cvec, kbias, gmat, _window_bias_c(), jnp.full((H_B,), MASKED, f32))
    layers = []
    for l in range(DEPTH):
        lam_init = 0.8 - 0.6 * math.exp(-0.3 * l)
        lv = lambda_a[l].astype(f32)
        lam = (jnp.exp(jnp.sum(lv[0] * lv[1])) - jnp.exp(jnp.sum(lv[2] * lv[3])) + lam_init).reshape(1)
        g2 = (subln_g_a[l].astype(f32) * (1.0 - lam_init))[:, None]
        layers.append(_layer_weights(l, w_in, norm_g, qk_gain_a, qk_gain_b, qk_gain_c)
                      + (lam, g2, _window_bias_b(rpb_b[l]), sink_c[l].astype(f32) * LOG2E,
                         w_proj_a[l].astype(bf16), w_proj_b[l].astype(bf16), w_proj_c[l].astype(bf16),
                         w_out[l].astype(bf16)))
    return (_trunk(x_prompt, layers, consts), _trunk(x_sample, layers, consts))
```

```python
import functools
import math

import numpy as np
import jax
import jax.numpy as jnp
from jax import lax
from jax.experimental import pallas as pl
from jax.experimental.pallas import tpu as pltpu

D_MODEL = 1024
DEPTH = 2
HEAD_DIM = 64
EPS = 1e-6
H_A = 4
W_A = H_A * 2 * HEAD_DIM
H_B = 8
W_B = H_B * HEAD_DIM
GRID_W = 64
MAX_KH = 8
KW = 16
H_C = 8
KV_C = 2
W_C = H_C * HEAD_DIM
WINDOW = 128

LOG2E = math.log2(math.e)
QK_SCALE = HEAD_DIM ** -0.5
MASKED = -1e30

LANES = 128
TM = 512
TQ_A = 256
TK_A = TM
TQ_W = 256
WIN_B = (MAX_KH + 4) * GRID_W
WIN_C = TQ_W + 2 * WINDOW
N_BIAS_LANES = 3
VMEM_LIMIT_BYTES = 56 * 1024 * 1024

_TOK_SIZES = (("ka", W_A), ("za", W_A), ("qb", W_B), ("kb", W_B), ("vb", W_B), ("zb", W_B),
              ("qc", W_C), ("kc", 4 * HEAD_DIM), ("vc", 4 * HEAD_DIM), ("zc", W_C),
              ("ga", D_MODEL), ("gb", D_MODEL), ("gc", D_MODEL))
_TOK_OFF = {}
_o = 0
for _n, _s in _TOK_SIZES:
    _TOK_OFF[_n] = (_o, _o + _s)
    _o += _s
TOK_COLS = _o


def _alibi_slopes(n):
    return [2.0 ** (-8.0 * (i + 1) / n) for i in range(n)]


def _params(**kw):
    return pltpu.CompilerParams(vmem_limit_bytes=VMEM_LIMIT_BYTES, **kw)


def _resident(block_shape, index_map):
    return pl.BlockSpec(block_shape, index_map, pipeline_mode=pl.Buffered(1))


def _inproj_kernel(x_ref, ng_ref, wtok_ref, wfeat_ref, gmat_ref, gains_ref, gfeat_ref,
                   qat_ref, ka_ref, vat_ref, za_ref, qb_ref, kb_ref, vb_ref, zb_ref,
                   qc_ref, kc_ref, vc_ref, zc_ref, g_ref):
    bf16, f32 = jnp.bfloat16, jnp.float32
    x = x_ref[0]
    ms = jnp.mean(x * x, axis=-1, keepdims=True)
    h = (x * lax.rsqrt(ms + EPS) * ng_ref[...]).astype(bf16)

    def proj(name):
        c0, c1 = _TOK_OFF[name]
        return jnp.dot(h, wtok_ref[:, c0:c1], preferred_element_type=f32)

    def normed(name, gain_row):
        p = proj(name)
        w = p.shape[-1]
        ss = jnp.dot((p * p).astype(bf16), gmat_ref[:w, :w], preferred_element_type=f32)
        return (p * lax.rsqrt(ss + EPS) * gains_ref[gain_row:gain_row + 1, :w]).astype(bf16)

    ka_ref[0] = normed("ka", 0)
    za_ref[0] = proj("za").astype(bf16)
    qb_ref[0] = normed("qb", 1)
    kb_ref[0] = normed("kb", 2)
    vb_ref[0] = proj("vb").astype(bf16)
    zb_ref[0] = proj("zb").astype(bf16)
    qc_ref[0] = normed("qc", 3)
    kc_ref[0] = normed("kc", 4)
    vc_ref[0] = proj("vc").astype(bf16)
    zc_ref[0] = proj("zc").astype(bf16)
    for i, name in enumerate(("ga", "gb", "gc")):
        g_ref[0, :, i * D_MODEL:(i + 1) * D_MODEL] = proj(name).astype(bf16)

    pf = lax.dot_general(wfeat_ref[...], h, (((1,), (1,)), ((), ())), preferred_element_type=f32)
    for g in range(W_A // HEAD_DIM):
        blk = pf[g * HEAD_DIM:(g + 1) * HEAD_DIM]
        r = lax.rsqrt(jnp.mean(blk * blk, axis=0, keepdims=True) + EPS)
        qat_ref[0, g * HEAD_DIM:(g + 1) * HEAD_DIM, :] = (blk * r * gfeat_ref[...]).astype(bf16)
    for hh in range(H_A):
        lo = W_A + hh * 2 * HEAD_DIM
        vat_ref[0, hh, 0] = pf[lo:lo + 2 * HEAD_DIM].astype(bf16)


def _inproj(x, ng, wtok, wfeat, gmat, gains, gfeat):
    B, T, _ = x.shape
    nt = T // TM
    bf16 = jnp.bfloat16
    tok = lambda w: pl.BlockSpec((1, TM, w), lambda b, t: (b, t, 0))
    const2 = lambda a: _resident(a.shape, lambda b, t: (0, 0))
    out_shape = (
        jax.ShapeDtypeStruct((B, W_A, T), bf16),
        jax.ShapeDtypeStruct((B, T, W_A), bf16),
        jax.ShapeDtypeStruct((B, H_A, nt, 2 * HEAD_DIM, TM), bf16),
        jax.ShapeDtypeStruct((B, T, W_A), bf16),
        jax.ShapeDtypeStruct((B, T, W_B), bf16),
        jax.ShapeDtypeStruct((B, T, W_B), bf16),
        jax.ShapeDtypeStruct((B, T, W_B), bf16),
        jax.ShapeDtypeStruct((B, T, W_B), bf16),
        jax.ShapeDtypeStruct((B, T, W_C), bf16),
        jax.ShapeDtypeStruct((B, T, 4 * HEAD_DIM), bf16),
        jax.ShapeDtypeStruct((B, T, 4 * HEAD_DIM), bf16),
        jax.ShapeDtypeStruct((B, T, W_C), bf16),
        jax.ShapeDtypeStruct((B, T, 3 * D_MODEL), bf16),
    )
    out_specs = (
        pl.BlockSpec((1, W_A, TM), lambda b, t: (b, 0, t)),
        tok(W_A),
        pl.BlockSpec((1, H_A, 1, 2 * HEAD_DIM, TM), lambda b, t: (b, 0, t, 0, 0)),
        tok(W_A), tok(W_B), tok(W_B), tok(W_B), tok(W_B), tok(W_C),
        tok(4 * HEAD_DIM), tok(4 * HEAD_DIM), tok(W_C), tok(3 * D_MODEL),
    )
    return pl.pallas_call(
        _inproj_kernel,
        grid=(B, nt),
        in_specs=[pl.BlockSpec((1, TM, D_MODEL), lambda b, t: (b, t, 0)),
                  const2(ng), const2(wtok), const2(wfeat), const2(gmat), const2(gains), const2(gfeat)],
        out_specs=out_specs,
        out_shape=out_shape,
        compiler_params=_params(dimension_semantics=("parallel", "parallel")),
        name="inproj",
    )(x, ng, wtok, wfeat, gmat, gains, gfeat)


def _diff_attn_kernel(c_ref, lam_ref, qt_ref, k_ref, vt_ref, kbias_ref, g2_ref, o_ref,
                      kp_ref, qv_ref, sa_ref, sb_ref, m_ref, l_ref, acc_ref, *, seq):
    bf16, f32 = jnp.bfloat16, jnp.float32
    h = pl.program_id(1)
    qi = pl.program_id(2)
    nk = seq // TK_A
    c = c_ref[h]

    @pl.when(qi == 0)
    def _():
        lane = lax.broadcasted_iota(jnp.int32, (TK_A, LANES), 1)
        kb0 = kbias_ref[0, 0].astype(f32)
        kb1 = kbias_ref[0, 1].astype(f32)

        def body(j, carry):
            rows = pl.ds(pl.multiple_of(j * TK_A, TK_A), TK_A)
            kblk = k_ref[0, rows, :].astype(f32)
            kp_ref[0, rows, :] = jnp.where(lane < HEAD_DIM, kblk, kb0).astype(bf16)
            kp_ref[1, rows, :] = jnp.where(lane >= HEAD_DIM, kblk, kb1).astype(bf16)
            return carry

        lax.fori_loop(0, nk, body, 0)

    row = lax.broadcasted_iota(jnp.int32, (2 * HEAD_DIM, TQ_A), 0)
    qblk = qt_ref[0].astype(f32)
    for sub in range(2):
        own = (row < HEAD_DIM) if sub == 0 else (row >= HEAD_DIM)
        b0 = HEAD_DIM if sub == 0 else 0
        brow = (row >= b0) & (row < b0 + N_BIAS_LANES)
        base = jnp.where(own, qblk, 0.0)
        for var, sigma in enumerate((1.0, -1.0, 0.0)):
            qv_ref[3 * sub + var] = jnp.where(brow, sigma, base).astype(bf16)

    m_ref[...] = jnp.full(m_ref.shape, MASKED, f32)
    l_ref[...] = jnp.zeros(l_ref.shape, f32)
    acc_ref[...] = jnp.zeros(acc_ref.shape, f32)

    i0 = qi * TQ_A
    jd = i0 // TK_A
    di = lax.broadcasted_iota(jnp.int32, (1, TQ_A), 1)

    def update(sub, t, rvec, vt):
        tmax = jnp.max(t, axis=0, keepdims=True)
        m_old = m_ref[sub]
        m_new = jnp.maximum(m_old, tmax + rvec)
        alpha = jnp.exp2(m_old - m_new)
        p = jnp.exp2(t - (m_new - rvec))
        l_ref[sub] = alpha * l_ref[sub] + jnp.sum(p, axis=0, keepdims=True)
        acc_ref[sub] = alpha * acc_ref[sub] + jnp.dot(vt, p.astype(bf16), preferred_element_type=f32)
        m_ref[sub] = m_new

    def block_of(jj):
        j = jj + (jj >= jd).astype(jnp.int32)
        return j, j > jd

    def produce(s_ref, jj):
        j, after = block_of(jj)
        var = after.astype(jnp.int32)
        rows = pl.ds(pl.multiple_of(j * TK_A, TK_A), TK_A)
        for sub in range(2):
            s_ref[sub] = jnp.dot(kp_ref[sub, rows, :], qv_ref[3 * sub + var], preferred_element_type=f32)

    def consume(s_ref, jj):
        j, after = block_of(jj)
        sgn = jnp.where(after, 1.0, -1.0).astype(f32)
        rvec = (sgn * c) * (i0 - j * TK_A + di).astype(f32)
        vt = vt_ref[0, 0, j]
        for sub in range(2):
            update(sub, s_ref[sub], rvec, vt)

    produce(sa_ref, 0)

    rows_d = pl.ds(pl.multiple_of(jd * TK_A, TK_A), TK_A)
    off = i0 - jd * TK_A
    dj = lax.broadcasted_iota(jnp.int32, (TK_A, TQ_A), 0)
    dq = lax.broadcasted_iota(jnp.int32, (TK_A, TQ_A), 1)
    dbias = -c * jnp.abs(off + dq - dj).astype(f32)
    vt_d = vt_ref[0, 0, jd]
    zero_r = jnp.zeros((1, TQ_A), f32)
    for sub in range(2):
        t = jnp.dot(kp_ref[sub, rows_d, :], qv_ref[3 * sub + 2], preferred_element_type=f32) + dbias
        update(sub, t, zero_r, vt_d)

    def body(i, carry):
        produce(sb_ref, 2 * i + 1)
        consume(sa_ref, 2 * i)
        produce(sa_ref, 2 * i + 2)
        consume(sb_ref, 2 * i + 1)
        return carry

    lax.fori_loop(0, (nk - 2) // 2, body, 0)
    consume(sa_ref, nk - 2)

    o = acc_ref[0] / l_ref[0] - lam_ref[0] * (acc_ref[1] / l_ref[1])
    y = o * lax.rsqrt(jnp.mean(o * o, axis=0, keepdims=True) + EPS) * g2_ref[...]
    o_ref[0] = y.T.astype(o_ref.dtype)


def _diff_attn(cvec, lam, qat, ka, vat, kbias, g2):
    B, _, T = qat.shape
    nq = T // TQ_A
    nkb = T // TK_A
    smem = pl.BlockSpec(memory_space=pltpu.SMEM)
    return pl.pallas_call(
        functools.partial(_diff_attn_kernel, seq=T),
        grid=(B, H_A, nq),
        in_specs=[
            smem, smem,
            pl.BlockSpec((1, 2 * HEAD_DIM, TQ_A), lambda b, h, q: (b, h, q)),
            pl.BlockSpec((1, T, 2 * HEAD_DIM), lambda b, h, q: (b, 0, h)),
            pl.BlockSpec((1, 1, nkb, 2 * HEAD_DIM, TK_A), lambda b, h, q: (b, h, 0, 0, 0)),
            pl.BlockSpec((1, 2, TK_A, LANES), lambda b, h, q: (h, 0, 0, 0)),
            pl.BlockSpec((2 * HEAD_DIM, 1), lambda b, h, q: (0, 0)),
        ],
        out_specs=pl.BlockSpec((1, TQ_A, 2 * HEAD_DIM), lambda b, h, q: (b, q, h)),
        out_shape=jax.ShapeDtypeStruct((B, T, W_A), jnp.bfloat16),
        scratch_shapes=[
            pltpu.VMEM((2, T, LANES), jnp.bfloat16),
            pltpu.VMEM((6, 2 * HEAD_DIM, TQ_A), jnp.bfloat16),
            pltpu.VMEM((2, TK_A, TQ_A), jnp.float32),
            pltpu.VMEM((2, TK_A, TQ_A), jnp.float32),
            pltpu.VMEM((2, 1, TQ_A), jnp.float32),
            pltpu.VMEM((2, 1, TQ_A), jnp.float32),
            pltpu.VMEM((2, 2 * HEAD_DIM, TQ_A), jnp.float32),
        ],
        compiler_params=_params(dimension_semantics=("parallel", "parallel", "arbitrary")),
        name="diff_attn",
    )(cvec, lam, qat, ka, vat, kbias, g2)


def _win_attn_kernel(sink_ref, q_ref, k_ref, v_ref, bias_ref, o_ref, *, seq, win, lead):
    bf16, f32 = jnp.bfloat16, jnp.float32
    pair = pl.program_id(1)
    qi = pl.program_id(2)
    ws = jnp.clip(qi * TQ_W - lead, 0, seq - win)
    rows = pl.ds(pl.multiple_of(ws, GRID_W), win)
    kw = k_ref[0, rows, :]
    vw = v_ref[0, rows, :]
    q = q_ref[0].astype(f32)
    lane = lax.broadcasted_iota(jnp.int32, (TQ_W, LANES), 1)
    outs = []
    for half in range(2):
        own = (lane < HEAD_DIM) if half == 0 else (lane >= HEAD_DIM)
        qm = jnp.where(own, q, 0.0).astype(bf16)
        s = lax.dot_general(qm, kw, (((1,), (1,)), ((), ())), preferred_element_type=f32)
        s = s + bias_ref[0, half]
        sink = sink_ref[2 * pair + half]
        m = jnp.maximum(jnp.max(s, axis=-1, keepdims=True), sink)
        p = jnp.exp2(s - m)
        l = jnp.sum(p, axis=-1, keepdims=True) + jnp.exp2(sink - m)
        outs.append(jnp.dot(p.astype(bf16), vw, preferred_element_type=f32) / l)
    o_ref[0] = jnp.where(lane < HEAD_DIM, outs[0], outs[1]).astype(o_ref.dtype)


def _win_attn(sink2, q, k, v, bias, *, win, lead, kv_group):
    B, T, _ = q.shape
    nq = T // TQ_W
    n_pairs = q.shape[-1] // LANES
    variant = lambda qi: jnp.where(qi == 0, 0, jnp.where(qi == nq - 1, 2, 1))
    return pl.pallas_call(
        functools.partial(_win_attn_kernel, seq=T, win=win, lead=lead),
        grid=(B, n_pairs, nq),
        in_specs=[
            pl.BlockSpec(memory_space=pltpu.SMEM),
            pl.BlockSpec((1, TQ_W, LANES), lambda b, p, qi: (b, qi, p)),
            pl.BlockSpec((1, T, LANES), lambda b, p, qi: (b, 0, p // kv_group)),
            pl.BlockSpec((1, T, LANES), lambda b, p, qi: (b, 0, p // kv_group)),
            pl.BlockSpec((1, 2, TQ_W, win), lambda b, p, qi: (variant(qi), p, 0, 0)),
        ],
        out_specs=pl.BlockSpec((1, TQ_W, LANES), lambda b, p, qi: (b, qi, p)),
        out_shape=jax.ShapeDtypeStruct(q.shape, jnp.bfloat16),
        compiler_params=_params(dimension_semantics=("parallel", "parallel", "arbitrary")),
        name="win_attn",
    )(sink2, q, k, v, bias)


def _merge_kernel(x_ref, oa_ref, ob_ref, oc_ref, za_ref, zb_ref, zc_ref, g_ref,
                  wa_ref, wb_ref, wc_ref, wo_ref, y_ref):
    bf16, f32 = jnp.bfloat16, jnp.float32

    def branch(o_ref, z_ref, w_ref, i):
        z = z_ref[0].astype(f32)
        u = o_ref[0].astype(f32) * (z * jax.nn.sigmoid(z))
        y = jnp.dot(u.astype(bf16), w_ref[...], preferred_element_type=f32)
        gate = g_ref[0, :, i * D_MODEL:(i + 1) * D_MODEL].astype(f32)
        return jax.nn.sigmoid(gate) * y

    m = branch(oa_ref, za_ref, wa_ref, 0) + branch(ob_ref, zb_ref, wb_ref, 1) + branch(oc_ref, zc_ref, wc_ref, 2)
    y_ref[0] = x_ref[0] + jnp.dot(m.astype(bf16), wo_ref[...], preferred_element_type=f32)


def _merge(x, oa, ob, oc, za, zb, zc, g, wa, wb, wc, wo):
    B, T, _ = x.shape
    tok = lambda w: pl.BlockSpec((1, TM, w), lambda b, t: (b, t, 0))
    const2 = lambda a: _resident(a.shape, lambda b, t: (0, 0))
    return pl.pallas_call(
        _merge_kernel,
        grid=(B, T // TM),
        in_specs=[tok(D_MODEL), tok(W_A), tok(W_B), tok(W_C), tok(W_A), tok(W_B), tok(W_C), tok(3 * D_MODEL),
                  const2(wa), const2(wb), const2(wc), const2(wo)],
        out_specs=tok(D_MODEL),
        out_shape=jax.ShapeDtypeStruct(x.shape, x.dtype),
        compiler_params=_params(dimension_semantics=("parallel", "parallel")),
        name="merge",
    )(x, oa, ob, oc, za, zb, zc, g, wa, wb, wc, wo)


def _split3_bf16(v):
    hi = v.astype(jnp.bfloat16)
    r1 = v - hi.astype(jnp.float32)
    mid = r1.astype(jnp.bfloat16)
    lo = (r1 - mid.astype(jnp.float32)).astype(jnp.bfloat16)
    return hi, mid, lo


def _diff_constants():
    c = np.asarray(_alibi_slopes(H_A), np.float32) * np.float32(LOG2E)
    col = jnp.asarray(c)[:, None] * jnp.arange(TK_A, dtype=jnp.float32)[None, :]
    pieces = jnp.stack(_split3_bf16(col), axis=-1)
    kbias = jnp.zeros((H_A, 2, TK_A, LANES), jnp.bfloat16)
    kbias = kbias.at[:, 0, :, HEAD_DIM:HEAD_DIM + N_BIAS_LANES].set(pieces)
    kbias = kbias.at[:, 1, :, 0:N_BIAS_LANES].set(pieces)
    return jnp.asarray(c), kbias


def _window_bias_c():
    slopes = np.asarray(_alibi_slopes(H_C), np.float64)
    di = np.arange(TQ_W)[:, None]
    dj = np.arange(WIN_C)[None, :]
    tiles = []
    for lead in (0, WINDOW, WIN_C - TQ_W):
        dist = np.abs(lead + di - dj)
        tile = np.where(dist <= WINDOW, -(slopes[:, None, None] * LOG2E) * dist[None], MASKED)
        tiles.append(tile)
    return jnp.asarray(np.stack(tiles).astype(np.float32))


def _window_bias_b(rpb):
    q_rows = TQ_W // GRID_W
    w_rows = WIN_B // GRID_W
    qc = np.arange(GRID_W)[:, None]
    kc = np.arange(GRID_W)[None, :]
    cstart = np.clip(qc - KW // 2, 0, GRID_W - KW)
    col_ok = (kc >= cstart) & (kc < cstart + KW)
    dc = kc - qc + (KW - 1)
    col_sel = (dc[:, :, None] == np.arange(2 * KW - 1)) & col_ok[:, :, None]
    qr = np.arange(q_rows)[:, None]
    wr = np.arange(w_rows)[None, :]
    row_sel, row_oks = [], []
    for lead_rows, rs in ((0, 0 * qr), (MAX_KH // 2, qr), (w_rows - q_rows, 0 * qr + MAX_KH // 2)):
        row_ok = (wr >= rs) & (wr < rs + MAX_KH)
        dr = wr - (lead_rows + qr) + (MAX_KH - 1)
        row_sel.append((dr[:, :, None] == np.arange(2 * MAX_KH - 1)) & row_ok[:, :, None])
        row_oks.append(row_ok)
    row_sel = jnp.asarray(np.stack(row_sel), jnp.float32)
    vals = jnp.einsum("vrwd,hde,cke->vhrcwk", row_sel, rpb.astype(jnp.float32) * LOG2E,
                      jnp.asarray(col_sel, jnp.float32), precision=lax.Precision.HIGHEST)
    ok = np.stack(row_oks)[:, None, :, None, :, None] & col_ok[None, None, None, :, None, :]
    return jnp.where(jnp.asarray(ok), vals, MASKED).reshape(3, H_B, TQ_W, WIN_B)


def _layer_weights(l, w_in, norm_g, qk_gain_a, qk_gain_b, qk_gain_c):
    bf16, f32 = jnp.bfloat16, jnp.float32
    w = w_in[l]
    sizes = (W_A,) * 4 + (W_B,) * 4 + (W_C, KV_C * HEAD_DIM, KV_C * HEAD_DIM, W_C) + (D_MODEL,) * 3
    offs = np.concatenate([[0], np.cumsum(sizes)])
    names = ("qa", "ka", "va", "za", "qb", "kb", "vb", "zb", "qc", "kc", "vc", "zc", "ga", "gb", "gc")
    cols = {n: w[:, offs[i]:offs[i + 1]] for i, n in enumerate(names)}
    dup = lambda a: jnp.concatenate([a[:, :HEAD_DIM], a[:, :HEAD_DIM], a[:, HEAD_DIM:], a[:, HEAD_DIM:]], axis=1)
    cols["kc"] = dup(cols["kc"])
    cols["vc"] = dup(cols["vc"])
    wtok = jnp.concatenate([cols[n] for n, _ in _TOK_SIZES], axis=1).astype(bf16)
    wfeat = jnp.concatenate([cols["qa"], cols["va"]], axis=1).T.astype(bf16)
    qscale = QK_SCALE * LOG2E
    tile8 = lambda g: jnp.tile(g.astype(f32), W_A // HEAD_DIM)
    gains = jnp.stack([tile8(qk_gain_a[l, 1]), tile8(qk_gain_b[l, 0]) * qscale, tile8(qk_gain_b[l, 1]),
                       tile8(qk_gain_c[l, 0]) * qscale, tile8(qk_gain_c[l, 1]),
                       jnp.zeros((W_A,), f32), jnp.zeros((W_A,), f32), jnp.zeros((W_A,), f32)])
    gfeat = (qk_gain_a[l, 0].astype(f32) * qscale)[:, None]
    ng = norm_g[l].astype(f32)[None, :]
    return ng, wtok, wfeat, gains, gfeat


def _trunk(x, layers, consts):
    cvec, kbias, gmat, bias_c, no_sink = consts
    for lw in layers:
        (ng, wtok, wfeat, gains, gfeat, lam, g2, bias_b, sink2, wa, wb, wc, wo) = lw
        (qat, ka, vat, za, qb, kb, vb, zb, qc, kc, vc, zc, g) = _inproj(x, ng, wtok, wfeat, gmat, gains, gfeat)
        oa = _diff_attn(cvec, lam, qat, ka, vat, kbias, g2)
        ob = _win_attn(no_sink, qb, kb, vb, bias_b, win=WIN_B, lead=(MAX_KH // 2) * GRID_W, kv_group=1)
        oc = _win_attn(sink2, qc, kc, vc, bias_c, win=WIN_C, lead=WINDOW, kv_group=2)
        x = _merge(x, oa, ob, oc, za, zb, zc, g, wa, wb, wc, wo)
    return x


def kernel(x_prompt, x_sample, norm_g, w_in, qk_gain_a, lambda_a, subln_g_a, qk_gain_b, rpb_b,
           qk_gain_c, sink_c, w_proj_a, w_proj_b, w_proj_c, w_out):
    bf16, f32 = jnp.bfloat16, jnp.float32
    for x in (x_prompt, x_sample):
        T = x.shape[1]
        assert T % (2 * TK_A) == 0 and T >= 3 * TQ_W and T >= WIN_B and x.shape[2] == D_MODEL
    cvec, kbias = _diff_constants()
    gidx = np.arange(W_A) // HEAD_DIM
    gmat = jnp.asarray((gidx[:, None] == gidx[None, :]).astype(np.float32) / HEAD_DIM, bf16)
    consts = (cvec, kbias, gmat, _window_bias_c(), jnp.full((H_B,), MASKED, f32))
    layers = []
    for l in range(DEPTH):
        lam_init = 0.8 - 0.6 * math.exp(-0.3 * l)
        lv = lambda_a[l].astype(f32)
        lam = (jnp.exp(jnp.sum(lv[0] * lv[1])) - jnp.exp(jnp.sum(lv[2] * lv[3])) + lam_init).reshape(1)
        g2 = (subln_g_a[l].astype(f32) * (1.0 - lam_init))[:, None]
        layers.append(_layer_weights(l, w_in, norm_g, qk_gain_a, qk_gain_b, qk_gain_c)
                      + (lam, g2, _window_bias_b(rpb_b[l]), sink_c[l].astype(f32) * LOG2E,
                         w_proj_a[l].astype(bf16), w_proj_b[l].astype(bf16), w_proj_c[l].astype(bf16),
                         w_out[l].astype(bf16)))
    return (_trunk(x_prompt, layers, consts), _trunk(x_sample, layers, consts))
```

```python
import functools
import math

import numpy as np
import jax
import jax.numpy as jnp
from jax import lax
from jax.experimental import pallas as pl
from jax.experimental.pallas import tpu as pltpu

D_MODEL = 1024
DEPTH = 2
HEAD_DIM = 64
EPS = 1e-6
H_A = 4
W_A = H_A * 2 * HEAD_DIM
H_B = 8
W_B = H_B * HEAD_DIM
GRID_W = 64
MAX_KH = 8
KW = 16
H_C = 8
KV_C = 2
W_C = H_C * HEAD_DIM
WINDOW = 128

LOG2E = math.log2(math.e)
QK_SCALE = HEAD_DIM ** -0.5
MASKED = -1e30

LANES = 128
TM = 512
TQ_A = 256
TK_A = TM
TQ_W = 256
WIN_B = (MAX_KH + 4) * GRID_W
WIN_C = TQ_W + 2 * WINDOW
N_BIAS_LANES = 3
VMEM_LIMIT_BYTES = 56 * 1024 * 1024

_TOK_SIZES = (("ka", W_A), ("za", W_A), ("qb", W_B), ("kb", W_B), ("vb", W_B), ("zb", W_B),
              ("qc", W_C), ("kc", 4 * HEAD_DIM), ("vc", 4 * HEAD_DIM), ("zc", W_C),
              ("ga", D_MODEL), ("gb", D_MODEL), ("gc", D_MODEL))
_TOK_OFF = {}
_o = 0
for _n, _s in _TOK_SIZES:
    _TOK_OFF[_n] = (_o, _o + _s)
    _o += _s
TOK_COLS = _o


def _alibi_slopes(n):
    return [2.0 ** (-8.0 * (i + 1) / n) for i in range(n)]


def _params(**kw):
    return pltpu.CompilerParams(vmem_limit_bytes=VMEM_LIMIT_BYTES, **kw)


def _resident(block_shape, index_map):
    return pl.BlockSpec(block_shape, index_map, pipeline_mode=pl.Buffered(1))


def _inproj_kernel(x_ref, ng_ref, wtok_ref, wfeat_ref, gmat_ref, gains_ref, gfeat_ref,
                   qat_ref, ka_ref, vat_ref, za_ref, qb_ref, kb_ref, vb_ref, zb_ref,
                   qc_ref, kc_ref, vc_ref, zc_ref, g_ref):
    bf16, f32 = jnp.bfloat16, jnp.float32
    x = x_ref[0]
    ms = jnp.mean(x * x, axis=-1, keepdims=True)
    h = (x * lax.rsqrt(ms + EPS) * ng_ref[...]).astype(bf16)

    def proj(name):
        c0, c1 = _TOK_OFF[name]
        return jnp.dot(h, wtok_ref[:, c0:c1], preferred_element_type=f32)

    def normed(name, gain_row):
        p = proj(name)
        w = p.shape[-1]
        ss = jnp.dot((p * p).astype(bf16), gmat_ref[:w, :w], preferred_element_type=f32)
        return (p * lax.rsqrt(ss + EPS) * gains_ref[gain_row:gain_row + 1, :w]).astype(bf16)

    ka_ref[0] = normed("ka", 0)
    za_ref[0] = proj("za").astype(bf16)
    qb_ref[0] = normed("qb", 1)
    kb_ref[0] = normed("kb", 2)
    vb_ref[0] = proj("vb").astype(bf16)
    zb_ref[0] = proj("zb").astype(bf16)
    qc_ref[0] = normed("qc", 3)
    kc_ref[0] = normed("kc", 4)
    vc_ref[0] = proj("vc").astype(bf16)
    zc_ref[0] = proj("zc").astype(bf16)
    for i, name in enumerate(("ga", "gb", "gc")):
        g_ref[0, :, i * D_MODEL:(i + 1) * D_MODEL] = proj(name).astype(bf16)

    pf = lax.dot_general(wfeat_ref[...], h, (((1,), (1,)), ((), ())), preferred_element_type=f32)
    for g in range(W_A // HEAD_DIM):
        blk = pf[g * HEAD_DIM:(g + 1) * HEAD_DIM]
        r = lax.rsqrt(jnp.mean(blk * blk, axis=0, keepdims=True) + EPS)
        qat_ref[0, g * HEAD_DIM:(g + 1) * HEAD_DIM, :] = (blk * r * gfeat_ref[...]).astype(bf16)
    for hh in range(H_A):
        lo = W_A + hh * 2 * HEAD_DIM
        vat_ref[0, hh, 0] = pf[lo:lo + 2 * HEAD_DIM].astype(bf16)


def _inproj(x, ng, wtok, wfeat, gmat, gains, gfeat):
    B, T, _ = x.shape
    nt = T // TM
    bf16 = jnp.bfloat16
    tok = lambda w: pl.BlockSpec((1, TM, w), lambda b, t: (b, t, 0))
    const2 = lambda a: _resident(a.shape, lambda b, t: (0, 0))
    out_shape = (
        jax.ShapeDtypeStruct((B, W_A, T), bf16),
        jax.ShapeDtypeStruct((B, T, W_A), bf16),
        jax.ShapeDtypeStruct((B, H_A, nt, 2 * HEAD_DIM, TM), bf16),
        jax.ShapeDtypeStruct((B, T, W_A), bf16),
        jax.ShapeDtypeStruct((B, T, W_B), bf16),
        jax.ShapeDtypeStruct((B, T, W_B), bf16),
        jax.ShapeDtypeStruct((B, T, W_B), bf16),
        jax.ShapeDtypeStruct((B, T, W_B), bf16),
        jax.ShapeDtypeStruct((B, T, W_C), bf16),
        jax.ShapeDtypeStruct((B, T, 4 * HEAD_DIM), bf16),
        jax.ShapeDtypeStruct((B, T, 4 * HEAD_DIM), bf16),
        jax.ShapeDtypeStruct((B, T, W_C), bf16),
        jax.ShapeDtypeStruct((B, T, 3 * D_MODEL), bf16),
    )
    out_specs = (
        pl.BlockSpec((1, W_A, TM), lambda b, t: (b, 0, t)),
        tok(W_A),
        pl.BlockSpec((1, H_A, 1, 2 * HEAD_DIM, TM), lambda b, t: (b, 0, t, 0, 0)),
        tok(W_A), tok(W_B), tok(W_B), tok(W_B), tok(W_B), tok(W_C),
        tok(4 * HEAD_DIM), tok(4 * HEAD_DIM), tok(W_C), tok(3 * D_MODEL),
    )
    return pl.pallas_call(
        _inproj_kernel,
        grid=(B, nt),
        in_specs=[pl.BlockSpec((1, TM, D_MODEL), lambda b, t: (b, t, 0)),
                  const2(ng), const2(wtok), const2(wfeat), const2(gmat), const2(gains), const2(gfeat)],
        out_specs=out_specs,
        out_shape=out_shape,
        compiler_params=_params(dimension_semantics=("parallel", "parallel")),
        name="inproj",
    )(x, ng, wtok, wfeat, gmat, gains, gfeat)


def _diff_attn_kernel(c_ref, lam_ref, qt_ref, k_ref, vt_ref, kbias_ref, g2_ref, o_ref,
                      kp_ref, qv_ref, sa_ref, sb_ref, m_ref, l_ref, acc_ref, *, seq):
    bf16, f32 = jnp.bfloat16, jnp.float32
    h = pl.program_id(1)
    qi = pl.program_id(2)
    nk = seq // TK_A
    c = c_ref[h]

    @pl.when(qi == 0)
    def _():
        lane = lax.broadcasted_iota(jnp.int32, (TK_A, LANES), 1)
        kb0 = kbias_ref[0, 0].astype(f32)
        kb1 = kbias_ref[0, 1].astype(f32)

        def body(j, carry):
            rows = pl.ds(pl.multiple_of(j * TK_A, TK_A), TK_A)
            kblk = k_ref[0, rows, :].astype(f32)
            kp_ref[0, rows, :] = jnp.where(lane < HEAD_DIM, kblk, kb0).astype(bf16)
            kp_ref[1, rows, :] = jnp.where(lane >= HEAD_DIM, kblk, kb1).astype(bf16)
            return carry

        lax.fori_loop(0, nk, body, 0)

    row = lax.broadcasted_iota(jnp.int32, (2 * HEAD_DIM, TQ_A), 0)
    qblk = qt_ref[0].astype(f32)
    for sub in range(2):
        own = (row < HEAD_DIM) if sub == 0 else (row >= HEAD_DIM)
        b0 = HEAD_DIM if sub == 0 else 0
        brow = (row >= b0) & (row < b0 + N_BIAS_LANES)
        base = jnp.where(own, qblk, 0.0)
        for var, sigma in enumerate((1.0, -1.0, 0.0)):
            qv_ref[3 * sub + var] = jnp.where(brow, sigma, base).astype(bf16)

    m_ref[...] = jnp.full(m_ref.shape, MASKED, f32)
    l_ref[...] = jnp.zeros(l_ref.shape, f32)
    acc_ref[...] = jnp.zeros(acc_ref.shape, f32)

    i0 = qi * TQ_A
    jd = i0 // TK_A
    di = lax.broadcasted_iota(jnp.int32, (1, TQ_A), 1)

    def update(sub, t, rvec, vt):
        tmax = jnp.max(t, axis=0, keepdims=True)
        m_old = m_ref[sub]
        m_new = jnp.maximum(m_old, tmax + rvec)
        alpha = jnp.exp2(m_old - m_new)
        p = jnp.exp2(t - (m_new - rvec))
        l_ref[sub] = alpha * l_ref[sub] + jnp.sum(p, axis=0, keepdims=True)
        acc_ref[sub] = alpha * acc_ref[sub] + jnp.dot(vt, p.astype(bf16), preferred_element_type=f32)
        m_ref[sub] = m_new

    def block_of(step):
        jj = step - 1
        j = jnp.where(step == 0, jd, jj + (jj >= jd).astype(jnp.int32))
        return j, j > jd

    def produce(s_ref, step):
        j, after = block_of(step)
        var = after.astype(jnp.int32)
        rows = pl.ds(pl.multiple_of(j * TK_A, TK_A), TK_A)
        for sub in range(2):
            s_ref[sub] = jnp.dot(kp_ref[sub, rows, :], qv_ref[3 * sub + var], preferred_element_type=f32)

    def consume(s_ref, step):
        j, after = block_of(step)
        sgn_c = jnp.where(j == jd, 0.0, jnp.where(after, c, -c)).astype(f32)
        rvec = sgn_c * (i0 - j * TK_A + di).astype(f32)
        vt = vt_ref[0, 0, j]
        for sub in range(2):
            update(sub, s_ref[sub], rvec, vt)

    rows_d = pl.ds(pl.multiple_of(jd * TK_A, TK_A), TK_A)
    off = i0 - jd * TK_A
    dj = lax.broadcasted_iota(jnp.int32, (TK_A, TQ_A), 0)
    dq = lax.broadcasted_iota(jnp.int32, (TK_A, TQ_A), 1)
    dbias = -c * jnp.abs(off + dq - dj).astype(f32)
    for sub in range(2):
        sa_ref[sub] = jnp.dot(kp_ref[sub, rows_d, :], qv_ref[3 * sub + 2], preferred_element_type=f32) + dbias

    def four_steps(s0, produce_last):
        produce(sb_ref, s0 + 1)
        consume(sa_ref, s0)
        produce(sa_ref, s0 + 2)
        consume(sb_ref, s0 + 1)
        produce(sb_ref, s0 + 3)
        consume(sa_ref, s0 + 2)
        if produce_last:
            produce(sa_ref, s0 + 4)
        consume(sb_ref, s0 + 3)

    def body(i, carry):
        four_steps(4 * i, True)
        return carry

    lax.fori_loop(0, nk // 4 - 1, body, 0)
    four_steps(nk - 4, False)

    o = acc_ref[0] / l_ref[0] - lam_ref[0] * (acc_ref[1] / l_ref[1])
    y = o * lax.rsqrt(jnp.mean(o * o, axis=0, keepdims=True) + EPS) * g2_ref[...]
    o_ref[0] = y.T.astype(o_ref.dtype)


def _diff_attn(cvec, lam, qat, ka, vat, kbias, g2):
    B, _, T = qat.shape
    nq = T // TQ_A
    nkb = T // TK_A
    smem = pl.BlockSpec(memory_space=pltpu.SMEM)
    return pl.pallas_call(
        functools.partial(_diff_attn_kernel, seq=T),
        grid=(B, H_A, nq),
        in_specs=[
            smem, smem,
            pl.BlockSpec((1, 2 * HEAD_DIM, TQ_A), lambda b, h, q: (b, h, q)),
            pl.BlockSpec((1, T, 2 * HEAD_DIM), lambda b, h, q: (b, 0, h)),
            pl.BlockSpec((1, 1, nkb, 2 * HEAD_DIM, TK_A), lambda b, h, q: (b, h, 0, 0, 0)),
            pl.BlockSpec((1, 2, TK_A, LANES), lambda b, h, q: (h, 0, 0, 0)),
            pl.BlockSpec((2 * HEAD_DIM, 1), lambda b, h, q: (0, 0)),
        ],
        out_specs=pl.BlockSpec((1, TQ_A, 2 * HEAD_DIM), lambda b, h, q: (b, q, h)),
        out_shape=jax.ShapeDtypeStruct((B, T, W_A), jnp.bfloat16),
        scratch_shapes=[
            pltpu.VMEM((2, T, LANES), jnp.bfloat16),
            pltpu.VMEM((6, 2 * HEAD_DIM, TQ_A), jnp.bfloat16),
            pltpu.VMEM((2, TK_A, TQ_A), jnp.float32),
            pltpu.VMEM((2, TK_A, TQ_A), jnp.float32),
            pltpu.VMEM((2, 1, TQ_A), jnp.float32),
            pltpu.VMEM((2, 1, TQ_A), jnp.float32),
            pltpu.VMEM((2, 2 * HEAD_DIM, TQ_A), jnp.float32),
        ],
        compiler_params=_params(dimension_semantics=("parallel", "parallel", "arbitrary")),
        name="diff_attn",
    )(cvec, lam, qat, ka, vat, kbias, g2)


def _win_attn_kernel(sink_ref, q_ref, k_ref, v_ref, bias_ref, o_ref, *, seq, win, lead):
    bf16, f32 = jnp.bfloat16, jnp.float32
    pair = pl.program_id(1)
    qi = pl.program_id(2)
    ws = jnp.clip(qi * TQ_W - lead, 0, seq - win)
    rows = pl.ds(pl.multiple_of(ws, GRID_W), win)
    kw = k_ref[0, rows, :]
    vw = v_ref[0, rows, :]
    q = q_ref[0].astype(f32)
    lane = lax.broadcasted_iota(jnp.int32, (TQ_W, LANES), 1)
    halves = (0, 1)
    qms = [jnp.where((lane < HEAD_DIM) if hf == 0 else (lane >= HEAD_DIM), q, 0.0).astype(bf16) for hf in halves]
    ss = [lax.dot_general(qms[hf], kw, (((1,), (1,)), ((), ())), preferred_element_type=f32) + bias_ref[0, hf]
          for hf in halves]
    sinks = [sink_ref[2 * pair + hf] for hf in halves]
    ms = [jnp.maximum(jnp.max(ss[hf], axis=-1, keepdims=True), sinks[hf]) for hf in halves]
    ps = [jnp.exp2(ss[hf] - ms[hf]) for hf in halves]
    ls = [jnp.sum(ps[hf], axis=-1, keepdims=True) + jnp.exp2(sinks[hf] - ms[hf]) for hf in halves]
    outs = [jnp.dot(ps[hf].astype(bf16), vw, preferred_element_type=f32) / ls[hf] for hf in halves]
    o_ref[0] = jnp.where(lane < HEAD_DIM, outs[0], outs[1]).astype(o_ref.dtype)


def _win_attn(sink2, q, k, v, bias, *, win, lead, kv_group):
    B, T, _ = q.shape
    nq = T // TQ_W
    n_pairs = q.shape[-1] // LANES
    variant = lambda qi: jnp.where(qi == 0, 0, jnp.where(qi == nq - 1, 2, 1))
    return pl.pallas_call(
        functools.partial(_win_attn_kernel, seq=T, win=win, lead=lead),
        grid=(B, n_pairs, nq),
        in_specs=[
            pl.BlockSpec(memory_space=pltpu.SMEM),
            pl.BlockSpec((1, TQ_W, LANES), lambda b, p, qi: (b, qi, p)),
            pl.BlockSpec((1, T, LANES), lambda b, p, qi: (b, 0, p // kv_group)),
            pl.BlockSpec((1, T, LANES), lambda b, p, qi: (b, 0, p // kv_group)),
            pl.BlockSpec((1, 2, TQ_W, win), lambda b, p, qi: (variant(qi), p, 0, 0)),
        ],
        out_specs=pl.BlockSpec((1, TQ_W, LANES), lambda b, p, qi: (b, qi, p)),
        out_shape=jax.ShapeDtypeStruct(q.shape, jnp.bfloat16),
        compiler_params=_params(dimension_semantics=("parallel", "parallel", "arbitrary")),
        name="win_attn",
    )(sink2, q, k, v, bias)


def _merge_kernel(x_ref, oa_ref, ob_ref, oc_ref, za_ref, zb_ref, zc_ref, g_ref,
                  wa_ref, wb_ref, wc_ref, wo_ref, y_ref):
    bf16, f32 = jnp.bfloat16, jnp.float32

    def branch(o_ref, z_ref, w_ref, i):
        z = z_ref[0].astype(f32)
        u = o_ref[0].astype(f32) * (z * jax.nn.sigmoid(z))
        y = jnp.dot(u.astype(bf16), w_ref[...], preferred_element_type=f32)
        gate = g_ref[0, :, i * D_MODEL:(i + 1) * D_MODEL].astype(f32)
        return jax.nn.sigmoid(gate) * y

    m = branch(oa_ref, za_ref, wa_ref, 0) + branch(ob_ref, zb_ref, wb_ref, 1) + branch(oc_ref, zc_ref, wc_ref, 2)
    y_ref[0] = x_ref[0] + jnp.dot(m.astype(bf16), wo_ref[...], preferred_element_type=f32)


def _merge(x, oa, ob, oc, za, zb, zc, g, wa, wb, wc, wo):
    B, T, _ = x.shape
    tok = lambda w: pl.BlockSpec((1, TM, w), lambda b, t: (b, t, 0))
    const2 = lambda a: _resident(a.shape, lambda b, t: (0, 0))
    return pl.pallas_call(
        _merge_kernel,
        grid=(B, T // TM),
        in_specs=[tok(D_MODEL), tok(W_A), tok(W_B), tok(W_C), tok(W_A), tok(W_B), tok(W_C), tok(3 * D_MODEL),
                  const2(wa), const2(wb), const2(wc), const2(wo)],
        out_specs=tok(D_MODEL),
        out_shape=jax.ShapeDtypeStruct(x.shape, x.dtype),
        compiler_params=_params(dimension_semantics=("parallel", "parallel")),
        name="merge",
    )(x, oa, ob, oc, za, zb, zc, g, wa, wb, wc, wo)


def _split3_bf16(v):
    hi = v.astype(jnp.bfloat16)
    r1 = v - hi.astype(jnp.float32)
    mid = r1.astype(jnp.bfloat16)
    lo = (r1 - mid.astype(jnp.float32)).astype(jnp.bfloat16)
    return hi, mid, lo


def _diff_constants():
    c = np.asarray(_alibi_slopes(H_A), np.float32) * np.float32(LOG2E)
    col = jnp.asarray(c)[:, None] * jnp.arange(TK_A, dtype=jnp.float32)[None, :]
    pieces = jnp.stack(_split3_bf16(col), axis=-1)
    kbias = jnp.zeros((H_A, 2, TK_A, LANES), jnp.bfloat16)
    kbias = kbias.at[:, 0, :, HEAD_DIM:HEAD_DIM + N_BIAS_LANES].set(pieces)
    kbias = kbias.at[:, 1, :, 0:N_BIAS_LANES].set(pieces)
    return jnp.asarray(c), kbias


def _window_bias_c():
    slopes = np.asarray(_alibi_slopes(H_C), np.float64)
    di = np.arange(TQ_W)[:, None]
    dj = np.arange(WIN_C)[None, :]
    tiles = []
    for lead in (0, WINDOW, WIN_C - TQ_W):
        dist = np.abs(lead + di - dj)
        tile = np.where(dist <= WINDOW, -(slopes[:, None, None] * LOG2E) * dist[None], MASKED)
        tiles.append(tile)
    return jnp.asarray(np.stack(tiles).astype(np.float32))


def _window_bias_b(rpb):
    q_rows = TQ_W // GRID_W
    w_rows = WIN_B // GRID_W
    qc = np.arange(GRID_W)[:, None]
    kc = np.arange(GRID_W)[None, :]
    cstart = np.clip(qc - KW // 2, 0, GRID_W - KW)
    col_ok = (kc >= cstart) & (kc < cstart + KW)
    dc = kc - qc + (KW - 1)
    col_sel = (dc[:, :, None] == np.arange(2 * KW - 1)) & col_ok[:, :, None]
    qr = np.arange(q_rows)[:, None]
    wr = np.arange(w_rows)[None, :]
    row_sel, row_oks = [], []
    for lead_rows, rs in ((0, 0 * qr), (MAX_KH // 2, qr), (w_rows - q_rows, 0 * qr + MAX_KH // 2)):
        row_ok = (wr >= rs) & (wr < rs + MAX_KH)
        dr = wr - (lead_rows + qr) + (MAX_KH - 1)
        row_sel.append((dr[:, :, None] == np.arange(2 * MAX_KH - 1)) & row_ok[:, :, None])
        row_oks.append(row_ok)
    row_sel = jnp.asarray(np.stack(row_sel), jnp.float32)
    vals = jnp.einsum("vrwd,hde,cke->vhrcwk", row_sel, rpb.astype(jnp.float32) * LOG2E,
                      jnp.asarray(col_sel, jnp.float32), precision=lax.Precision.HIGHEST)
    ok = np.stack(row_oks)[:, None, :, None, :, None] & col_ok[None, None, None, :, None, :]
    return jnp.where(jnp.asarray(ok), vals, MASKED).reshape(3, H_B, TQ_W, WIN_B)


def _layer_weights(l, w_in, norm_g, qk_gain_a, qk_gain_b, qk_gain_c):
    bf16, f32 = jnp.bfloat16, jnp.float32
    w = w_in[l]
    sizes = (W_A,) * 4 + (W_B,) * 4 + (W_C, KV_C * HEAD_DIM, KV_C * HEAD_DIM, W_C) + (D_MODEL,) * 3
    offs = np.concatenate([[0], np.cumsum(sizes)])
    names = ("qa", "ka", "va", "za", "qb", "kb", "vb", "zb", "qc", "kc", "vc", "zc", "ga", "gb", "gc")
    cols = {n: w[:, offs[i]:offs[i + 1]] for i, n in enumerate(names)}
    dup = lambda a: jnp.concatenate([a[:, :HEAD_DIM], a[:, :HEAD_DIM], a[:, HEAD_DIM:], a[:, HEAD_DIM:]], axis=1)
    cols["kc"] = dup(cols["kc"])
    cols["vc"] = dup(cols["vc"])
    wtok = jnp.concatenate([cols[n] for n, _ in _TOK_SIZES], axis=1).astype(bf16)
    wfeat = jnp.concatenate([cols["qa"], cols["va"]], axis=1).T.astype(bf16)
    qscale = QK_SCALE * LOG2E
    tile8 = lambda g: jnp.tile(g.astype(f32), W_A // HEAD_DIM)
    gains = jnp.stack([tile8(qk_gain_a[l, 1]), tile8(qk_gain_b[l, 0]) * qscale, tile8(qk_gain_b[l, 1]),
                       tile8(qk_gain_c[l, 0]) * qscale, tile8(qk_gain_c[l, 1]),
                       jnp.zeros((W_A,), f32), jnp.zeros((W_A,), f32), jnp.zeros((W_A,), f32)])
    gfeat = (qk_gain_a[l, 0].astype(f32) * qscale)[:, None]
    ng = norm_g[l].astype(f32)[None, :]
    return ng, wtok, wfeat, gains, gfeat


def _trunk(x, layers, consts):
    cvec, kbias, gmat, bias_c, no_sink = consts
    for lw in layers:
        (ng, wtok, wfeat, gains, gfeat, lam, g2, bias_b, sink2, wa, wb, wc, wo) = lw
        (qat, ka, vat, za, qb, kb, vb, zb, qc, kc, vc, zc, g) = _inproj(x, ng, wtok, wfeat, gmat, gains, gfeat)
        oa = _diff_attn(cvec, lam, qat, ka, vat, kbias, g2)
        ob = _win_attn(no_sink, qb, kb, vb, bias_b, win=WIN_B, lead=(MAX_KH // 2) * GRID_W, kv_group=1)
        oc = _win_attn(sink2, qc, kc, vc, bias_c, win=WIN_C, lead=WINDOW, kv_group=2)
        x = _merge(x, oa, ob, oc, za, zb, zc, g, wa, wb, wc, wo)
    return x


def kernel(x_prompt, x_sample, norm_g, w_in, qk_gain_a, lambda_a, subln_g_a, qk_gain_b, rpb_b,
           qk_gain_c, sink_c, w_proj_a, w_proj_b, w_proj_c, w_out):
    bf16, f32 = jnp.bfloat16, jnp.float32
    for x in (x_prompt, x_sample):
        T = x.shape[1]
        assert T % (4 * TK_A) == 0 and T >= 3 * TQ_W and T >= WIN_B and x.shape[2] == D_MODEL
    cvec, kbias = _diff_constants()
    gidx = np.arange(W_A) // HEAD_DIM
    gmat = jnp.asarray((gidx[:, None] == gidx[None, :]).astype(np.float32) / HEAD_DIM, bf16)
    consts = (cvec, kbias, gmat, _window_bias_c(), jnp.full((H_B,), MASKED, f32))
    layers = []
    for l in range(DEPTH):
        lam_init = 0.8 - 0.6 * math.exp(-0.3 * l)
        lv = lambda_a[l].astype(f32)
        lam = (jnp.exp(jnp.sum(lv[0] * lv[1])) - jnp.exp(jnp.sum(lv[2] * lv[3])) + lam_init).reshape(1)
        g2 = (subln_g_a[l].astype(f32) * (1.0 - lam_init))[:, None]
        layers.append(_layer_weights(l, w_in, norm_g, qk_gain_a, qk_gain_b, qk_gain_c)
                      + (lam, g2, _window_bias_b(rpb_b[l]), sink_c[l].astype(f32) * LOG2E,
                         w_proj_a[l].astype(bf16), w_proj_b[l].astype(bf16), w_proj_c[l].astype(bf16),
                         w_out[l].astype(bf16)))
    return (_trunk(x_prompt, layers, consts), _trunk(x_sample, layers, consts))
```

```python
import functools
import math

import numpy as np
import jax
import jax.numpy as jnp
from jax import lax
from jax.experimental import pallas as pl
from jax.experimental.pallas import tpu as pltpu

D_MODEL = 1024
DEPTH = 2
HEAD_DIM = 64
EPS = 1e-6
H_A = 4
W_A = H_A * 2 * HEAD_DIM
H_B = 8
W_B = H_B * HEAD_DIM
GRID_W = 64
MAX_KH = 8
KW = 16
H_C = 8
KV_C = 2
W_C = H_C * HEAD_DIM
WINDOW = 128

LOG2E = math.log2(math.e)
QK_SCALE = HEAD_DIM ** -0.5
MASKED = -1e30

LANES = 128
TM = 512
TQ_A = 256
TK_A = TM
STEPS_A = 8
TQ_W = 256
QBLOCKS_W = 2
WIN_B = (MAX_KH + 4) * GRID_W
WIN_C = TQ_W + 2 * WINDOW
N_BIAS_LANES = 3
BF16_ROWS = 16
VT_ROWS = 2 * HEAD_DIM + BF16_ROWS
VMEM_LIMIT_BYTES = 56 * 1024 * 1024

_TOK_SIZES = (("ka", W_A), ("za", W_A), ("qb", W_B), ("kb", W_B), ("vb", W_B), ("zb", W_B),
              ("qc", W_C), ("kc", 4 * HEAD_DIM), ("vc", 4 * HEAD_DIM), ("zc", W_C),
              ("ga", D_MODEL), ("gb", D_MODEL), ("gc", D_MODEL))
_TOK_OFF = {}
_o = 0
for _n, _s in _TOK_SIZES:
    _TOK_OFF[_n] = (_o, _o + _s)
    _o += _s
TOK_COLS = _o


def _alibi_slopes(n):
    return [2.0 ** (-8.0 * (i + 1) / n) for i in range(n)]


def _params(**kw):
    return pltpu.CompilerParams(vmem_limit_bytes=VMEM_LIMIT_BYTES, **kw)


def _resident(block_shape, index_map):
    return pl.BlockSpec(block_shape, index_map, pipeline_mode=pl.Buffered(1))


def _inproj_kernel(x_ref, ng_ref, wtok_ref, wfeat_ref, gmat_ref, gains_ref, gfeat_ref,
                   qat_ref, ka_ref, vat_ref, za_ref, qb_ref, kb_ref, vb_ref, zb_ref,
                   qc_ref, kc_ref, vc_ref, zc_ref, g_ref):
    bf16, f32 = jnp.bfloat16, jnp.float32
    x = x_ref[0]
    ms = jnp.mean(x * x, axis=-1, keepdims=True)
    h = (x * lax.rsqrt(ms + EPS) * ng_ref[...]).astype(bf16)

    def proj(name):
        c0, c1 = _TOK_OFF[name]
        return jnp.dot(h, wtok_ref[:, c0:c1], preferred_element_type=f32)

    def normed(name, gain_row):
        p = proj(name)
        w = p.shape[-1]
        ss = jnp.dot((p * p).astype(bf16), gmat_ref[:w, :w], preferred_element_type=f32)
        return (p * lax.rsqrt(ss + EPS) * gains_ref[gain_row:gain_row + 1, :w]).astype(bf16)

    ka_ref[0] = normed("ka", 0)
    za_ref[0] = proj("za").astype(bf16)
    qb_ref[0] = normed("qb", 1)
    kb_ref[0] = normed("kb", 2)
    vb_ref[0] = proj("vb").astype(bf16)
    zb_ref[0] = proj("zb").astype(bf16)
    qc_ref[0] = normed("qc", 3)
    kc_ref[0] = normed("kc", 4)
    vc_ref[0] = proj("vc").astype(bf16)
    zc_ref[0] = proj("zc").astype(bf16)
    for i, name in enumerate(("ga", "gb", "gc")):
        g_ref[0, :, i * D_MODEL:(i + 1) * D_MODEL] = proj(name).astype(bf16)

    pf = lax.dot_general(wfeat_ref[...], h, (((1,), (1,)), ((), ())), preferred_element_type=f32)
    for g in range(W_A // HEAD_DIM):
        blk = pf[g * HEAD_DIM:(g + 1) * HEAD_DIM]
        r = lax.rsqrt(jnp.mean(blk * blk, axis=0, keepdims=True) + EPS)
        qat_ref[0, g * HEAD_DIM:(g + 1) * HEAD_DIM, :] = (blk * r * gfeat_ref[...]).astype(bf16)
    pad_row = lax.broadcasted_iota(jnp.int32, (BF16_ROWS, TM), 0)
    ones_row = jnp.where(pad_row == 0, 1.0, 0.0).astype(bf16)
    for hh in range(H_A):
        lo = W_A + hh * 2 * HEAD_DIM
        vat_ref[0, hh, 0, :2 * HEAD_DIM, :] = pf[lo:lo + 2 * HEAD_DIM].astype(bf16)
        vat_ref[0, hh, 0, 2 * HEAD_DIM:, :] = ones_row


def _inproj(x, ng, wtok, wfeat, gmat, gains, gfeat):
    B, T, _ = x.shape
    nt = T // TM
    bf16 = jnp.bfloat16
    tok = lambda w: pl.BlockSpec((1, TM, w), lambda b, t: (b, t, 0))
    const2 = lambda a: _resident(a.shape, lambda b, t: (0, 0))
    out_shape = (
        jax.ShapeDtypeStruct((B, W_A, T), bf16),
        jax.ShapeDtypeStruct((B, T, W_A), bf16),
        jax.ShapeDtypeStruct((B, H_A, nt, VT_ROWS, TM), bf16),
        jax.ShapeDtypeStruct((B, T, W_A), bf16),
        jax.ShapeDtypeStruct((B, T, W_B), bf16),
        jax.ShapeDtypeStruct((B, T, W_B), bf16),
        jax.ShapeDtypeStruct((B, T, W_B), bf16),
        jax.ShapeDtypeStruct((B, T, W_B), bf16),
        jax.ShapeDtypeStruct((B, T, W_C), bf16),
        jax.ShapeDtypeStruct((B, T, 4 * HEAD_DIM), bf16),
        jax.ShapeDtypeStruct((B, T, 4 * HEAD_DIM), bf16),
        jax.ShapeDtypeStruct((B, T, W_C), bf16),
        jax.ShapeDtypeStruct((B, T, 3 * D_MODEL), bf16),
    )
    out_specs = (
        pl.BlockSpec((1, W_A, TM), lambda b, t: (b, 0, t)),
        tok(W_A),
        pl.BlockSpec((1, H_A, 1, VT_ROWS, TM), lambda b, t: (b, 0, t, 0, 0)),
        tok(W_A), tok(W_B), tok(W_B), tok(W_B), tok(W_B), tok(W_C),
        tok(4 * HEAD_DIM), tok(4 * HEAD_DIM), tok(W_C), tok(3 * D_MODEL),
    )
    return pl.pallas_call(
        _inproj_kernel,
        grid=(B, nt),
        in_specs=[pl.BlockSpec((1, TM, D_MODEL), lambda b, t: (b, t, 0)),
                  const2(ng), const2(wtok), const2(wfeat), const2(gmat), const2(gains), const2(gfeat)],
        out_specs=out_specs,
        out_shape=out_shape,
        compiler_params=_params(dimension_semantics=("parallel", "parallel")),
        name="inproj",
    )(x, ng, wtok, wfeat, gmat, gains, gfeat)


def _steps_per_iteration(nk):
    return STEPS_A if nk >= 2 * STEPS_A else STEPS_A // 2


def _diff_attn_kernel(c_ref, lam_ref, trips_ref, qt_ref, k_ref, vt_ref, kbias_ref, g2_ref, o_ref,
                      kp_ref, qv_ref, sa_ref, sb_ref, ta_ref, tb_ref, m_ref, acc_ref, *, seq):
    bf16, f32 = jnp.bfloat16, jnp.float32
    h = pl.program_id(1)
    qi = pl.program_id(2)
    nk = seq // TK_A
    n_steps = _steps_per_iteration(nk)
    c = c_ref[h]

    @pl.when(qi == 0)
    def _():
        lane = lax.broadcasted_iota(jnp.int32, (TK_A, LANES), 1)
        kb0 = kbias_ref[0, 0].astype(f32)
        kb1 = kbias_ref[0, 1].astype(f32)

        def body(j, carry):
            rows = pl.ds(pl.multiple_of(j * TK_A, TK_A), TK_A)
            kblk = k_ref[0, rows, :].astype(f32)
            kp_ref[0, rows, :] = jnp.where(lane < HEAD_DIM, kblk, kb0).astype(bf16)
            kp_ref[1, rows, :] = jnp.where(lane >= HEAD_DIM, kblk, kb1).astype(bf16)
            return carry

        lax.fori_loop(0, nk, body, 0)

    row = lax.broadcasted_iota(jnp.int32, (2 * HEAD_DIM, TQ_A), 0)
    qblk = qt_ref[0].astype(f32)
    for sub in range(2):
        own = (row < HEAD_DIM) if sub == 0 else (row >= HEAD_DIM)
        b0 = HEAD_DIM if sub == 0 else 0
        brow = (row >= b0) & (row < b0 + N_BIAS_LANES)
        base = jnp.where(own, qblk, 0.0)
        for var, sigma in enumerate((1.0, -1.0, 0.0)):
            qv_ref[3 * sub + var] = jnp.where(brow, sigma, base).astype(bf16)

    m_ref[...] = jnp.full(m_ref.shape, MASKED, f32)
    acc_ref[...] = jnp.zeros(acc_ref.shape, f32)

    i0 = qi * TQ_A
    jd = i0 // TK_A
    di = lax.broadcasted_iota(jnp.int32, (1, TQ_A), 1)

    def update(sub, t, tmax, rvec, vt):
        m_old = m_ref[sub]
        m_new = jnp.maximum(m_old, tmax + rvec)
        alpha = jnp.exp2(m_old - m_new)
        p = jnp.exp2(t - (m_new - rvec))
        acc_ref[sub] = alpha * acc_ref[sub] + jnp.dot(vt, p.astype(bf16), preferred_element_type=f32)
        m_ref[sub] = m_new

    def put_scores(s_ref, t_ref, sub, t):
        s_ref[sub] = t
        t_ref[sub] = jnp.max(t, axis=0, keepdims=True)

    def block_of(step):
        jj = step - 1
        j = jnp.where(step == 0, jd, jj + (jj >= jd).astype(jnp.int32))
        return j, j > jd

    def produce(bufs, step):
        s_ref, t_ref = bufs
        j, after = block_of(step)
        var = after.astype(jnp.int32)
        rows = pl.ds(pl.multiple_of(j * TK_A, TK_A), TK_A)
        for sub in range(2):
            put_scores(s_ref, t_ref, sub,
                       jnp.dot(kp_ref[sub, rows, :], qv_ref[3 * sub + var], preferred_element_type=f32))

    def consume(bufs, step):
        s_ref, t_ref = bufs
        j, after = block_of(step)
        sgn_c = jnp.where(j == jd, 0.0, jnp.where(after, c, -c)).astype(f32)
        rvec = sgn_c * (i0 - j * TK_A + di).astype(f32)
        vt = vt_ref[0, 0, j]
        for sub in range(2):
            update(sub, s_ref[sub], t_ref[sub], rvec, vt)

    buf_a = (sa_ref, ta_ref)
    buf_b = (sb_ref, tb_ref)

    rows_d = pl.ds(pl.multiple_of(jd * TK_A, TK_A), TK_A)
    off = i0 - jd * TK_A
    dj = lax.broadcasted_iota(jnp.int32, (TK_A, TQ_A), 0)
    dq = lax.broadcasted_iota(jnp.int32, (TK_A, TQ_A), 1)
    dbias = -c * jnp.abs(off + dq - dj).astype(f32)
    for sub in range(2):
        put_scores(sa_ref, ta_ref, sub,
                   jnp.dot(kp_ref[sub, rows_d, :], qv_ref[3 * sub + 2], preferred_element_type=f32) + dbias)

    def steps(s0, produce_last):
        for k in range(0, n_steps, 2):
            produce(buf_b, s0 + k + 1)
            consume(buf_a, s0 + k)
            if produce_last or k + 2 < n_steps:
                produce(buf_a, s0 + k + 2)
            consume(buf_b, s0 + k + 1)

    def body(i, carry):
        steps(n_steps * i, True)
        return carry

    lax.fori_loop(0, trips_ref[0], body, 0)
    steps(nk - n_steps, False)

    nf = 2 * HEAD_DIM
    o = (acc_ref[0, :nf, :] / acc_ref[0, nf:nf + 1, :]
         - lam_ref[0] * (acc_ref[1, :nf, :] / acc_ref[1, nf:nf + 1, :]))
    y = o * lax.rsqrt(jnp.mean(o * o, axis=0, keepdims=True) + EPS) * g2_ref[...]
    o_ref[0] = y.T.astype(o_ref.dtype)


def _diff_attn(cvec, lam, qat, ka, vat, kbias, g2):
    B, _, T = qat.shape
    nq = T // TQ_A
    nkb = T // TK_A
    smem = pl.BlockSpec(memory_space=pltpu.SMEM)
    trips = jnp.full((1,), nkb // _steps_per_iteration(nkb) - 1, jnp.int32)
    return pl.pallas_call(
        functools.partial(_diff_attn_kernel, seq=T),
        grid=(B, H_A, nq),
        in_specs=[
            smem, smem, smem,
            pl.BlockSpec((1, 2 * HEAD_DIM, TQ_A), lambda b, h, q: (b, h, q)),
            pl.BlockSpec((1, T, 2 * HEAD_DIM), lambda b, h, q: (b, 0, h)),
            pl.BlockSpec((1, 1, nkb, VT_ROWS, TK_A), lambda b, h, q: (b, h, 0, 0, 0)),
            pl.BlockSpec((1, 2, TK_A, LANES), lambda b, h, q: (h, 0, 0, 0)),
            pl.BlockSpec((2 * HEAD_DIM, 1), lambda b, h, q: (0, 0)),
        ],
        out_specs=pl.BlockSpec((1, TQ_A, 2 * HEAD_DIM), lambda b, h, q: (b, q, h)),
        out_shape=jax.ShapeDtypeStruct((B, T, W_A), jnp.bfloat16),
        scratch_shapes=[
            pltpu.VMEM((2, T, LANES), jnp.bfloat16),
            pltpu.VMEM((6, 2 * HEAD_DIM, TQ_A), jnp.bfloat16),
            pltpu.VMEM((2, TK_A, TQ_A), jnp.float32),
            pltpu.VMEM((2, TK_A, TQ_A), jnp.float32),
            pltpu.VMEM((2, 1, TQ_A), jnp.float32),
            pltpu.VMEM((2, 1, TQ_A), jnp.float32),
            pltpu.VMEM((2, 1, TQ_A), jnp.float32),
            pltpu.VMEM((2, VT_ROWS, TQ_A), jnp.float32),
        ],
        compiler_params=_params(dimension_semantics=("parallel", "parallel", "arbitrary")),
        name="diff_attn",
    )(cvec, lam, trips, qat, ka, vat, kbias, g2)


def _win_attn_kernel(sink_ref, q_ref, k_ref, v_ref, *rest, seq, win, lead):
    bf16, f32 = jnp.bfloat16, jnp.float32
    bias_refs, o_ref = rest[:QBLOCKS_W], rest[QBLOCKS_W]
    pair = pl.program_id(1)
    step = pl.program_id(2)
    lane = lax.broadcasted_iota(jnp.int32, (TQ_W, LANES), 1)
    jobs = [(qb, hf) for qb in range(QBLOCKS_W) for hf in range(2)]
    kws, vws, qs = [], [], []
    for qb in range(QBLOCKS_W):
        ws = jnp.clip((step * QBLOCKS_W + qb) * TQ_W - lead, 0, seq - win)
        rows = pl.ds(pl.multiple_of(ws, GRID_W), win)
        kws.append(k_ref[0, rows, :])
        vws.append(v_ref[0, rows, :])
        qs.append(q_ref[0, qb * TQ_W:(qb + 1) * TQ_W, :].astype(f32))
    sinks = [sink_ref[2 * pair + hf] for hf in range(2)]
    qms = [jnp.where((lane < HEAD_DIM) if hf == 0 else (lane >= HEAD_DIM), qs[qb], 0.0).astype(bf16)
           for qb, hf in jobs]
    ss = [lax.dot_general(qms[n], kws[qb], (((1,), (1,)), ((), ())), preferred_element_type=f32)
          + bias_refs[qb][0, hf] for n, (qb, hf) in enumerate(jobs)]
    ms = [jnp.maximum(jnp.max(ss[n], axis=-1, keepdims=True), sinks[hf]) for n, (qb, hf) in enumerate(jobs)]
    ps = [jnp.exp2(ss[n] - ms[n]) for n in range(len(jobs))]
    ls = [jnp.sum(ps[n], axis=-1, keepdims=True) + jnp.exp2(sinks[hf] - ms[n]) for n, (qb, hf) in enumerate(jobs)]
    outs = [jnp.dot(ps[n].astype(bf16), vws[qb], preferred_element_type=f32) / ls[n]
            for n, (qb, hf) in enumerate(jobs)]
    for qb in range(QBLOCKS_W):
        o_ref[0, qb * TQ_W:(qb + 1) * TQ_W, :] = jnp.where(lane < HEAD_DIM, outs[2 * qb], outs[2 * qb + 1]
                                                           ).astype(o_ref.dtype)


def _win_attn(sink2, q, k, v, bias, *, win, lead, kv_group):
    B, T, _ = q.shape
    nq = T // TQ_W
    n_pairs = q.shape[-1] // LANES
    tq_step = QBLOCKS_W * TQ_W

    def bias_spec(qb):
        def index_map(b, p, s):
            qi = s * QBLOCKS_W + qb
            return (jnp.where(qi == 0, 0, jnp.where(qi == nq - 1, 2, 1)), p, 0, 0)
        return pl.BlockSpec((1, 2, TQ_W, win), index_map)

    return pl.pallas_call(
        functools.partial(_win_attn_kernel, seq=T, win=win, lead=lead),
        grid=(B, n_pairs, T // tq_step),
        in_specs=[
            pl.BlockSpec(memory_space=pltpu.SMEM),
            pl.BlockSpec((1, tq_step, LANES), lambda b, p, s: (b, s, p)),
            pl.BlockSpec((1, T, LANES), lambda b, p, s: (b, 0, p // kv_group)),
            pl.BlockSpec((1, T, LANES), lambda b, p, s: (b, 0, p // kv_group)),
        ] + [bias_spec(qb) for qb in range(QBLOCKS_W)],
        out_specs=pl.BlockSpec((1, tq_step, LANES), lambda b, p, s: (b, s, p)),
        out_shape=jax.ShapeDtypeStruct(q.shape, jnp.bfloat16),
        compiler_params=_params(dimension_semantics=("parallel", "parallel", "arbitrary")),
        name="win_attn",
    )(sink2, q, k, v, *([bias] * QBLOCKS_W))


def _merge_kernel(x_ref, oa_ref, ob_ref, oc_ref, za_ref, zb_ref, zc_ref, g_ref,
                  wa_ref, wb_ref, wc_ref, wo_ref, y_ref):
    bf16, f32 = jnp.bfloat16, jnp.float32

    def branch(o_ref, z_ref, w_ref, i):
        z = z_ref[0].astype(f32)
        u = o_ref[0].astype(f32) * (z * jax.nn.sigmoid(z))
        y = jnp.dot(u.astype(bf16), w_ref[...], preferred_element_type=f32)
        gate = g_ref[0, :, i * D_MODEL:(i + 1) * D_MODEL].astype(f32)
        return jax.nn.sigmoid(gate) * y

    m = branch(oa_ref, za_ref, wa_ref, 0) + branch(ob_ref, zb_ref, wb_ref, 1) + branch(oc_ref, zc_ref, wc_ref, 2)
    y_ref[0] = x_ref[0] + jnp.dot(m.astype(bf16), wo_ref[...], preferred_element_type=f32)


def _merge(x, oa, ob, oc, za, zb, zc, g, wa, wb, wc, wo):
    B, T, _ = x.shape
    tok = lambda w: pl.BlockSpec((1, TM, w), lambda b, t: (b, t, 0))
    const2 = lambda a: _resident(a.shape, lambda b, t: (0, 0))
    return pl.pallas_call(
        _merge_kernel,
        grid=(B, T // TM),
        in_specs=[tok(D_MODEL), tok(W_A), tok(W_B), tok(W_C), tok(W_A), tok(W_B), tok(W_C), tok(3 * D_MODEL),
                  const2(wa), const2(wb), const2(wc), const2(wo)],
        out_specs=tok(D_MODEL),
        out_shape=jax.ShapeDtypeStruct(x.shape, x.dtype),
        compiler_params=_params(dimension_semantics=("parallel", "parallel")),
        name="merge",
    )(x, oa, ob, oc, za, zb, zc, g, wa, wb, wc, wo)


def _split3_bf16(v):
    hi = v.astype(jnp.bfloat16)
    r1 = v - hi.astype(jnp.float32)
    mid = r1.astype(jnp.bfloat16)
    lo = (r1 - mid.astype(jnp.float32)).astype(jnp.bfloat16)
    return hi, mid, lo


def _diff_constants():
    c = np.asarray(_alibi_slopes(H_A), np.float32) * np.float32(LOG2E)
    col = jnp.asarray(c)[:, None] * jnp.arange(TK_A, dtype=jnp.float32)[None, :]
    pieces = jnp.stack(_split3_bf16(col), axis=-1)
    kbias = jnp.zeros((H_A, 2, TK_A, LANES), jnp.bfloat16)
    kbias = kbias.at[:, 0, :, HEAD_DIM:HEAD_DIM + N_BIAS_LANES].set(pieces)
    kbias = kbias.at[:, 1, :, 0:N_BIAS_LANES].set(pieces)
    return jnp.asarray(c), kbias


def _window_bias_c():
    slopes = np.asarray(_alibi_slopes(H_C), np.float64)
    di = np.arange(TQ_W)[:, None]
    dj = np.arange(WIN_C)[None, :]
    tiles = []
    for lead in (0, WINDOW, WIN_C - TQ_W):
        dist = np.abs(lead + di - dj)
        tile = np.where(dist <= WINDOW, -(slopes[:, None, None] * LOG2E) * dist[None], MASKED)
        tiles.append(tile)
    return jnp.asarray(np.stack(tiles).astype(np.float32))


def _window_bias_b(rpb):
    q_rows = TQ_W // GRID_W
    w_rows = WIN_B // GRID_W
    qc = np.arange(GRID_W)[:, None]
    kc = np.arange(GRID_W)[None, :]
    cstart = np.clip(qc - KW // 2, 0, GRID_W - KW)
    col_ok = (kc >= cstart) & (kc < cstart + KW)
    dc = kc - qc + (KW - 1)
    col_sel = (dc[:, :, None] == np.arange(2 * KW - 1)) & col_ok[:, :, None]
    qr = np.arange(q_rows)[:, None]
    wr = np.arange(w_rows)[None, :]
    row_sel, row_oks = [], []
    for lead_rows, rs in ((0, 0 * qr), (MAX_KH // 2, qr), (w_rows - q_rows, 0 * qr + MAX_KH // 2)):
        row_ok = (wr >= rs) & (wr < rs + MAX_KH)
        dr = wr - (lead_rows + qr) + (MAX_KH - 1)
        row_sel.append((dr[:, :, None] == np.arange(2 * MAX_KH - 1)) & row_ok[:, :, None])
        row_oks.append(row_ok)
    row_sel = jnp.asarray(np.stack(row_sel), jnp.float32)
    vals = jnp.einsum("vrwd,hde,cke->vhrcwk", row_sel, rpb.astype(jnp.float32) * LOG2E,
                      jnp.asarray(col_sel, jnp.float32), precision=lax.Precision.HIGHEST)
    ok = np.stack(row_oks)[:, None, :, None, :, None] & col_ok[None, None, None, :, None, :]
    return jnp.where(jnp.asarray(ok), vals, MASKED).reshape(3, H_B, TQ_W, WIN_B)


def _layer_weights(l, w_in, norm_g, qk_gain_a, qk_gain_b, qk_gain_c):
    bf16, f32 = jnp.bfloat16, jnp.float32
    w = w_in[l]
    sizes = (W_A,) * 4 + (W_B,) * 4 + (W_C, KV_C * HEAD_DIM, KV_C * HEAD_DIM, W_C) + (D_MODEL,) * 3
    offs = np.concatenate([[0], np.cumsum(sizes)])
    names = ("qa", "ka", "va", "za", "qb", "kb", "vb", "zb", "qc", "kc", "vc", "zc", "ga", "gb", "gc")
    cols = {n: w[:, offs[i]:offs[i + 1]] for i, n in enumerate(names)}
    dup = lambda a: jnp.concatenate([a[:, :HEAD_DIM], a[:, :HEAD_DIM], a[:, HEAD_DIM:], a[:, HEAD_DIM:]], axis=1)
    cols["kc"] = dup(cols["kc"])
    cols["vc"] = dup(cols["vc"])
    wtok = jnp.concatenate([cols[n] for n, _ in _TOK_SIZES], axis=1).astype(bf16)
    wfeat = jnp.concatenate([cols["qa"], cols["va"]], axis=1).T.astype(bf16)
    qscale = QK_SCALE * LOG2E
    tile8 = lambda g: jnp.tile(g.astype(f32), W_A // HEAD_DIM)
    gains = jnp.stack([tile8(qk_gain_a[l, 1]), tile8(qk_gain_b[l, 0]) * qscale, tile8(qk_gain_b[l, 1]),
                       tile8(qk_gain_c[l, 0]) * qscale, tile8(qk_gain_c[l, 1]),
                       jnp.zeros((W_A,), f32), jnp.zeros((W_A,), f32), jnp.zeros((W_A,), f32)])
    gfeat = (qk_gain_a[l, 0].astype(f32) * qscale)[:, None]
    ng = norm_g[l].astype(f32)[None, :]
    return ng, wtok, wfeat, gains, gfeat


def _trunk(x, layers, consts):
    cvec, kbias, gmat, bias_c, no_sink = consts
    for lw in layers:
        (ng, wtok, wfeat, gains, gfeat, lam, g2, bias_b, sink2, wa, wb, wc, wo) = lw
        (qat, ka, vat, za, qb, kb, vb, zb, qc, kc, vc, zc, g) = _inproj(x, ng, wtok, wfeat, gmat, gains, gfeat)
        oa = _diff_attn(cvec, lam, qat, ka, vat, kbias, g2)
        ob = _win_attn(no_sink, qb, kb, vb, bias_b, win=WIN_B, lead=(MAX_KH // 2) * GRID_W, kv_group=1)
        oc = _win_attn(sink2, qc, kc, vc, bias_c, win=WIN_C, lead=WINDOW, kv_group=2)
        x = _merge(x, oa, ob, oc, za, zb, zc, g, wa, wb, wc, wo)
    return x


def kernel(x_prompt, x_sample, norm_g, w_in, qk_gain_a, lambda_a, subln_g_a, qk_gain_b, rpb_b,
           qk_gain_c, sink_c, w_proj_a, w_proj_b, w_proj_c, w_out):
    bf16, f32 = jnp.bfloat16, jnp.float32
    for x in (x_prompt, x_sample):
        T = x.shape[1]
        assert T % (STEPS_A * TK_A) == 0 and T >= 3 * TQ_W and T >= WIN_B and x.shape[2] == D_MODEL
    cvec, kbias = _diff_constants()
    gidx = np.arange(W_A) // HEAD_DIM
    gmat = jnp.asarray((gidx[:, None] == gidx[None, :]).astype(np.float32) / HEAD_DIM, bf16)
    consts = (cvec, kbias, gmat, _window_bias_c(), jnp.full((H_B,), MASKED, f32))
    layers = []
    for l in range(DEPTH):
        lam_init = 0.8 - 0.6 * math.exp(-0.3 * l)
        lv = lambda_a[l].astype(f32)
        lam = (jnp.exp(jnp.sum(lv[0] * lv[1])) - jnp.exp(jnp.sum(lv[2] * lv[3])) + lam_init).reshape(1)
        g2 = (subln_g_a[l].astype(f32) * (1.0 - lam_init))[:, None]
        layers.append(_layer_weights(l, w_in, norm_g, qk_gain_a, qk_gain_b, qk_gain_c)
                      + (lam, g2, _window_bias_b(rpb_b[l]), sink_c[l].astype(f32) * LOG2E,
                         w_proj_a[l].astype(bf16), w_proj_b[l].astype(bf16), w_proj_c[l].astype(bf16),
                         w_out[l].astype(bf16)))
    return (_trunk(x_prompt, layers, consts), _trunk(x_sample, layers, consts))
```

```python
import functools
import math

import numpy as np
import jax
import jax.numpy as jnp
from jax import lax
from jax.experimental import pallas as pl
from jax.experimental.pallas import tpu as pltpu

D_MODEL = 1024
DEPTH = 2
HEAD_DIM = 64
EPS = 1e-6
H_A = 4
W_A = H_A * 2 * HEAD_DIM
H_B = 8
W_B = H_B * HEAD_DIM
GRID_W = 64
MAX_KH = 8
KW = 16
H_C = 8
KV_C = 2
W_C = H_C * HEAD_DIM
WINDOW = 128

LOG2E = math.log2(math.e)
QK_SCALE = HEAD_DIM ** -0.5
MASKED = -1e30

LANES = 128
TM = 512
NORM_TILE = 256
TQ_A = 512
TK_A = TM
STEPS_A = 8
TQ_W = 256
QBLOCKS_W = 2
WIN_B = (MAX_KH + 4) * GRID_W
WIN_C = TQ_W + 2 * WINDOW
N_BIAS_LANES = 3
BF16_ROWS = 16
VT_ROWS = 2 * HEAD_DIM + BF16_ROWS
VMEM_LIMIT_BYTES = 56 * 1024 * 1024

_TOK_SIZES = (("ka", W_A), ("za", W_A), ("qb", W_B), ("kb", W_B), ("vb", W_B), ("zb", W_B),
              ("qc", W_C), ("kc", 4 * HEAD_DIM), ("vc", 4 * HEAD_DIM), ("zc", W_C),
              ("ga", D_MODEL), ("gb", D_MODEL), ("gc", D_MODEL))
_TOK_OFF = {}
_o = 0
for _n, _s in _TOK_SIZES:
    _TOK_OFF[_n] = (_o, _o + _s)
    _o += _s
TOK_COLS = _o


def _alibi_slopes(n):
    return [2.0 ** (-8.0 * (i + 1) / n) for i in range(n)]


def _params(**kw):
    return pltpu.CompilerParams(vmem_limit_bytes=VMEM_LIMIT_BYTES, **kw)


def _resident(block_shape, index_map):
    return pl.BlockSpec(block_shape, index_map, pipeline_mode=pl.Buffered(1))


def _inproj_kernel(x_ref, ng_ref, wtok_ref, wfeat_ref, gmat_ref, gains_ref, gfeat_ref,
                   qat_ref, ka_ref, vat_ref, za_ref, qb_ref, kb_ref, vb_ref, zb_ref,
                   qc_ref, kc_ref, vc_ref, zc_ref, g_ref):
    bf16, f32 = jnp.bfloat16, jnp.float32
    x = x_ref[0]
    ms = jnp.mean(x * x, axis=-1, keepdims=True)
    h = (x * lax.rsqrt(ms + EPS) * ng_ref[...]).astype(bf16)

    def proj(name):
        c0, c1 = _TOK_OFF[name]
        return jnp.dot(h, wtok_ref[:, c0:c1], preferred_element_type=f32)

    def normed(name, gain_row):
        p = proj(name)
        w = p.shape[-1]
        pp = (p * p).astype(bf16)
        ss = jnp.concatenate([jnp.dot(pp[:, c0:c0 + NORM_TILE], gmat_ref[...], preferred_element_type=f32)
                              for c0 in range(0, w, NORM_TILE)], axis=1)
        return (p * lax.rsqrt(ss + EPS) * gains_ref[gain_row:gain_row + 1, :w]).astype(bf16)

    ka_ref[0] = normed("ka", 0)
    za_ref[0] = proj("za").astype(bf16)
    qb_ref[0] = normed("qb", 1)
    kb_ref[0] = normed("kb", 2)
    vb_ref[0] = proj("vb").astype(bf16)
    zb_ref[0] = proj("zb").astype(bf16)
    qc_ref[0] = normed("qc", 3)
    kc_ref[0] = normed("kc", 4)
    vc_ref[0] = proj("vc").astype(bf16)
    zc_ref[0] = proj("zc").astype(bf16)
    for i, name in enumerate(("ga", "gb", "gc")):
        g_ref[0, :, i * D_MODEL:(i + 1) * D_MODEL] = proj(name).astype(bf16)

    pf = lax.dot_general(wfeat_ref[...], h, (((1,), (1,)), ((), ())), preferred_element_type=f32)
    for g in range(W_A // HEAD_DIM):
        blk = pf[g * HEAD_DIM:(g + 1) * HEAD_DIM]
        r = lax.rsqrt(jnp.mean(blk * blk, axis=0, keepdims=True) + EPS)
        qat_ref[0, g * HEAD_DIM:(g + 1) * HEAD_DIM, :] = (blk * r * gfeat_ref[...]).astype(bf16)
    pad_row = lax.broadcasted_iota(jnp.int32, (BF16_ROWS, TM), 0)
    ones_row = jnp.where(pad_row == 0, 1.0, 0.0).astype(bf16)
    for hh in range(H_A):
        lo = W_A + hh * 2 * HEAD_DIM
        vat_ref[0, hh, 0, :2 * HEAD_DIM, :] = pf[lo:lo + 2 * HEAD_DIM].astype(bf16)
        vat_ref[0, hh, 0, 2 * HEAD_DIM:, :] = ones_row


def _inproj(x, ng, wtok, wfeat, gmat, gains, gfeat):
    B, T, _ = x.shape
    nt = T // TM
    bf16 = jnp.bfloat16
    tok = lambda w: pl.BlockSpec((1, TM, w), lambda b, t: (b, t, 0))
    const2 = lambda a: _resident(a.shape, lambda b, t: (0, 0))
    out_shape = (
        jax.ShapeDtypeStruct((B, W_A, T), bf16),
        jax.ShapeDtypeStruct((B, T, W_A), bf16),
        jax.ShapeDtypeStruct((B, H_A, nt, VT_ROWS, TM), bf16),
        jax.ShapeDtypeStruct((B, T, W_A), bf16),
        jax.ShapeDtypeStruct((B, T, W_B), bf16),
        jax.ShapeDtypeStruct((B, T, W_B), bf16),
        jax.ShapeDtypeStruct((B, T, W_B), bf16),
        jax.ShapeDtypeStruct((B, T, W_B), bf16),
        jax.ShapeDtypeStruct((B, T, W_C), bf16),
        jax.ShapeDtypeStruct((B, T, 4 * HEAD_DIM), bf16),
        jax.ShapeDtypeStruct((B, T, 4 * HEAD_DIM), bf16),
        jax.ShapeDtypeStruct((B, T, W_C), bf16),
        jax.ShapeDtypeStruct((B, T, 3 * D_MODEL), bf16),
    )
    out_specs = (
        pl.BlockSpec((1, W_A, TM), lambda b, t: (b, 0, t)),
        tok(W_A),
        pl.BlockSpec((1, H_A, 1, VT_ROWS, TM), lambda b, t: (b, 0, t, 0, 0)),
        tok(W_A), tok(W_B), tok(W_B), tok(W_B), tok(W_B), tok(W_C),
        tok(4 * HEAD_DIM), tok(4 * HEAD_DIM), tok(W_C), tok(3 * D_MODEL),
    )
    return pl.pallas_call(
        _inproj_kernel,
        grid=(B, nt),
        in_specs=[pl.BlockSpec((1, TM, D_MODEL), lambda b, t: (b, t, 0)),
                  const2(ng), const2(wtok), const2(wfeat), const2(gmat), const2(gains), const2(gfeat)],
        out_specs=out_specs,
        out_shape=out_shape,
        compiler_params=_params(dimension_semantics=("parallel", "parallel")),
        name="inproj",
    )(x, ng, wtok, wfeat, gmat, gains, gfeat)


def _steps_per_iteration(nk):
    return STEPS_A if nk >= 2 * STEPS_A else STEPS_A // 2


def _diff_attn_kernel(c_ref, lam_ref, trips_ref, qt_ref, k_ref, vt_ref, kbias_ref, g2_ref, o_ref,
                      kp_ref, qv_ref, sa_ref, sb_ref, ta_ref, tb_ref, m_ref, acc_ref, *, seq):
    bf16, f32 = jnp.bfloat16, jnp.float32
    h = pl.program_id(1)
    qi = pl.program_id(2)
    nk = seq // TK_A
    n_steps = _steps_per_iteration(nk)
    c = c_ref[h]

    @pl.when(qi == 0)
    def _():
        lane = lax.broadcasted_iota(jnp.int32, (TK_A, LANES), 1)
        kb0 = kbias_ref[0, 0].astype(f32)
        kb1 = kbias_ref[0, 1].astype(f32)

        def body(j, carry):
            rows = pl.ds(pl.multiple_of(j * TK_A, TK_A), TK_A)
            kblk = k_ref[0, rows, :].astype(f32)
            kp_ref[0, rows, :] = jnp.where(lane < HEAD_DIM, kblk, kb0).astype(bf16)
            kp_ref[1, rows, :] = jnp.where(lane >= HEAD_DIM, kblk, kb1).astype(bf16)
            return carry

        lax.fori_loop(0, nk, body, 0)

    row = lax.broadcasted_iota(jnp.int32, (2 * HEAD_DIM, TQ_A), 0)
    qblk = qt_ref[0].astype(f32)
    for sub in range(2):
        own = (row < HEAD_DIM) if sub == 0 else (row >= HEAD_DIM)
        b0 = HEAD_DIM if sub == 0 else 0
        brow = (row >= b0) & (row < b0 + N_BIAS_LANES)
        base = jnp.where(own, qblk, 0.0)
        for var, sigma in enumerate((1.0, -1.0, 0.0)):
            qv_ref[3 * sub + var] = jnp.where(brow, sigma, base).astype(bf16)

    m_ref[...] = jnp.full(m_ref.shape, MASKED, f32)
    acc_ref[...] = jnp.zeros(acc_ref.shape, f32)

    i0 = qi * TQ_A
    jd = i0 // TK_A
    di = lax.broadcasted_iota(jnp.int32, (1, TQ_A), 1)

    def update(sub, t, tmax, rvec, vt):
        m_old = m_ref[sub]
        m_new = jnp.maximum(m_old, tmax + rvec)
        alpha = jnp.exp2(m_old - m_new)
        p = jnp.exp2(t - (m_new - rvec))
        acc_ref[sub] = alpha * acc_ref[sub] + jnp.dot(vt, p.astype(bf16), preferred_element_type=f32)
        m_ref[sub] = m_new

    def put_scores(s_ref, t_ref, sub, t):
        s_ref[sub] = t
        t_ref[sub] = jnp.max(t, axis=0, keepdims=True)

    def block_of(step):
        jj = step - 1
        j = jnp.where(step == 0, jd, jj + (jj >= jd).astype(jnp.int32))
        return j, j > jd

    def produce(bufs, step):
        s_ref, t_ref = bufs
        j, after = block_of(step)
        var = after.astype(jnp.int32)
        rows = pl.ds(pl.multiple_of(j * TK_A, TK_A), TK_A)
        for sub in range(2):
            put_scores(s_ref, t_ref, sub,
                       jnp.dot(kp_ref[sub, rows, :], qv_ref[3 * sub + var], preferred_element_type=f32))

    def consume(bufs, step):
        s_ref, t_ref = bufs
        j, after = block_of(step)
        sgn_c = jnp.where(j == jd, 0.0, jnp.where(after, c, -c)).astype(f32)
        rvec = sgn_c * (i0 - j * TK_A + di).astype(f32)
        vt = vt_ref[0, 0, j]
        for sub in range(2):
            update(sub, s_ref[sub], t_ref[sub], rvec, vt)

    buf_a = (sa_ref, ta_ref)
    buf_b = (sb_ref, tb_ref)

    rows_d = pl.ds(pl.multiple_of(jd * TK_A, TK_A), TK_A)
    off = i0 - jd * TK_A
    dj = lax.broadcasted_iota(jnp.int32, (TK_A, TQ_A), 0)
    dq = lax.broadcasted_iota(jnp.int32, (TK_A, TQ_A), 1)
    dbias = -c * jnp.abs(off + dq - dj).astype(f32)
    for sub in range(2):
        put_scores(sa_ref, ta_ref, sub,
                   jnp.dot(kp_ref[sub, rows_d, :], qv_ref[3 * sub + 2], preferred_element_type=f32) + dbias)

    def steps(s0, produce_last):
        for k in range(0, n_steps, 2):
            produce(buf_b, s0 + k + 1)
            consume(buf_a, s0 + k)
            if produce_last or k + 2 < n_steps:
                produce(buf_a, s0 + k + 2)
            consume(buf_b, s0 + k + 1)

    def body(i, carry):
        steps(n_steps * i, True)
        return carry

    lax.fori_loop(0, trips_ref[0], body, 0)
    steps(nk - n_steps, False)

    nf = 2 * HEAD_DIM
    o = (acc_ref[0, :nf, :] / acc_ref[0, nf:nf + 1, :]
         - lam_ref[0] * (acc_ref[1, :nf, :] / acc_ref[1, nf:nf + 1, :]))
    y = o * lax.rsqrt(jnp.mean(o * o, axis=0, keepdims=True) + EPS) * g2_ref[...]
    o_ref[0] = y.T.astype(o_ref.dtype)


def _diff_attn(cvec, lam, qat, ka, vat, kbias, g2):
    B, _, T = qat.shape
    nq = T // TQ_A
    nkb = T // TK_A
    smem = pl.BlockSpec(memory_space=pltpu.SMEM)
    trips = jnp.full((1,), nkb // _steps_per_iteration(nkb) - 1, jnp.int32)
    return pl.pallas_call(
        functools.partial(_diff_attn_kernel, seq=T),
        grid=(B, H_A, nq),
        in_specs=[
            smem, smem, smem,
            pl.BlockSpec((1, 2 * HEAD_DIM, TQ_A), lambda b, h, q: (b, h, q)),
            pl.BlockSpec((1, T, 2 * HEAD_DIM), lambda b, h, q: (b, 0, h)),
            pl.BlockSpec((1, 1, nkb, VT_ROWS, TK_A), lambda b, h, q: (b, h, 0, 0, 0)),
            pl.BlockSpec((1, 2, TK_A, LANES), lambda b, h, q: (h, 0, 0, 0)),
            pl.BlockSpec((2 * HEAD_DIM, 1), lambda b, h, q: (0, 0)),
        ],
        out_specs=pl.BlockSpec((1, TQ_A, 2 * HEAD_DIM), lambda b, h, q: (b, q, h)),
        out_shape=jax.ShapeDtypeStruct((B, T, W_A), jnp.bfloat16),
        scratch_shapes=[
            pltpu.VMEM((2, T, LANES), jnp.bfloat16),
            pltpu.VMEM((6, 2 * HEAD_DIM, TQ_A), jnp.bfloat16),
            pltpu.VMEM((2, TK_A, TQ_A), jnp.float32),
            pltpu.VMEM((2, TK_A, TQ_A), jnp.float32),
            pltpu.VMEM((2, 1, TQ_A), jnp.float32),
            pltpu.VMEM((2, 1, TQ_A), jnp.float32),
            pltpu.VMEM((2, 1, TQ_A), jnp.float32),
            pltpu.VMEM((2, VT_ROWS, TQ_A), jnp.float32),
        ],
        compiler_params=_params(dimension_semantics=("parallel", "parallel", "arbitrary")),
        name="diff_attn",
    )(cvec, lam, trips, qat, ka, vat, kbias, g2)


def _win_attn_kernel(sink_ref, q_ref, k_ref, v_ref, *rest, seq, win, lead):
    bf16, f32 = jnp.bfloat16, jnp.float32
    bias_refs, o_ref = rest[:QBLOCKS_W], rest[QBLOCKS_W]
    pair = pl.program_id(1)
    step = pl.program_id(2)
    lane = lax.broadcasted_iota(jnp.int32, (TQ_W, LANES), 1)
    jobs = [(qb, hf) for qb in range(QBLOCKS_W) for hf in range(2)]
    kws, vws, qs = [], [], []
    for qb in range(QBLOCKS_W):
        ws = jnp.clip((step * QBLOCKS_W + qb) * TQ_W - lead, 0, seq - win)
        rows = pl.ds(pl.multiple_of(ws, GRID_W), win)
        kws.append(k_ref[0, rows, :])
        vws.append(v_ref[0, rows, :])
        qs.append(q_ref[0, qb * TQ_W:(qb + 1) * TQ_W, :].astype(f32))
    sinks = [sink_ref[2 * pair + hf] for hf in range(2)]
    qms = [jnp.where((lane < HEAD_DIM) if hf == 0 else (lane >= HEAD_DIM), qs[qb], 0.0).astype(bf16)
           for qb, hf in jobs]
    ss = [lax.dot_general(qms[n], kws[qb], (((1,), (1,)), ((), ())), preferred_element_type=f32)
          + bias_refs[qb][0, hf] for n, (qb, hf) in enumerate(jobs)]
    ms = [jnp.maximum(jnp.max(ss[n], axis=-1, keepdims=True), sinks[hf]) for n, (qb, hf) in enumerate(jobs)]
    ps = [jnp.exp2(ss[n] - ms[n]) for n in range(len(jobs))]
    ls = [jnp.sum(ps[n], axis=-1, keepdims=True) + jnp.exp2(sinks[hf] - ms[n]) for n, (qb, hf) in enumerate(jobs)]
    outs = [jnp.dot(ps[n].astype(bf16), vws[qb], preferred_element_type=f32) / ls[n]
            for n, (qb, hf) in enumerate(jobs)]
    for qb in range(QBLOCKS_W):
        o_ref[0, qb * TQ_W:(qb + 1) * TQ_W, :] = jnp.where(lane < HEAD_DIM, outs[2 * qb], outs[2 * qb + 1]
                                                           ).astype(o_ref.dtype)


def _win_attn(sink2, q, k, v, bias, *, win, lead, kv_group):
    B, T, _ = q.shape
    nq = T // TQ_W
    n_pairs = q.shape[-1] // LANES
    tq_step = QBLOCKS_W * TQ_W

    def bias_spec(qb):
        def index_map(b, p, s):
            qi = s * QBLOCKS_W + qb
            return (jnp.where(qi == 0, 0, jnp.where(qi == nq - 1, 2, 1)), p, 0, 0)
        return pl.BlockSpec((1, 2, TQ_W, win), index_map)

    return pl.pallas_call(
        functools.partial(_win_attn_kernel, seq=T, win=win, lead=lead),
        grid=(B, n_pairs, T // tq_step),
        in_specs=[
            pl.BlockSpec(memory_space=pltpu.SMEM),
            pl.BlockSpec((1, tq_step, LANES), lambda b, p, s: (b, s, p)),
            pl.BlockSpec((1, T, LANES), lambda b, p, s: (b, 0, p // kv_group)),
            pl.BlockSpec((1, T, LANES), lambda b, p, s: (b, 0, p // kv_group)),
        ] + [bias_spec(qb) for qb in range(QBLOCKS_W)],
        out_specs=pl.BlockSpec((1, tq_step, LANES), lambda b, p, s: (b, s, p)),
        out_shape=jax.ShapeDtypeStruct(q.shape, jnp.bfloat16),
        compiler_params=_params(dimension_semantics=("parallel", "parallel", "arbitrary")),
        name="win_attn",
    )(sink2, q, k, v, *([bias] * QBLOCKS_W))


def _merge_kernel(x_ref, oa_ref, ob_ref, oc_ref, za_ref, zb_ref, zc_ref, g_ref,
                  wa_ref, wb_ref, wc_ref, wo_ref, y_ref):
    bf16, f32 = jnp.bfloat16, jnp.float32

    def branch(o_ref, z_ref, w_ref, i):
        z = z_ref[0].astype(f32)
        u = o_ref[0].astype(f32) * (z * jax.nn.sigmoid(z))
        y = jnp.dot(u.astype(bf16), w_ref[...], preferred_element_type=f32)
        gate = g_ref[0, :, i * D_MODEL:(i + 1) * D_MODEL].astype(f32)
        return jax.nn.sigmoid(gate) * y

    m = branch(oa_ref, za_ref, wa_ref, 0) + branch(ob_ref, zb_ref, wb_ref, 1) + branch(oc_ref, zc_ref, wc_ref, 2)
    y_ref[0] = x_ref[0] + jnp.dot(m.astype(bf16), wo_ref[...], preferred_element_type=f32)


def _merge(x, oa, ob, oc, za, zb, zc, g, wa, wb, wc, wo):
    B, T, _ = x.shape
    tok = lambda w: pl.BlockSpec((1, TM, w), lambda b, t: (b, t, 0))
    const2 = lambda a: _resident(a.shape, lambda b, t: (0, 0))
    return pl.pallas_call(
        _merge_kernel,
        grid=(B, T // TM),
        in_specs=[tok(D_MODEL), tok(W_A), tok(W_B), tok(W_C), tok(W_A), tok(W_B), tok(W_C), tok(3 * D_MODEL),
                  const2(wa), const2(wb), const2(wc), const2(wo)],
        out_specs=tok(D_MODEL),
        out_shape=jax.ShapeDtypeStruct(x.shape, x.dtype),
        compiler_params=_params(dimension_semantics=("parallel", "parallel")),
        name="merge",
    )(x, oa, ob, oc, za, zb, zc, g, wa, wb, wc, wo)


def _split3_bf16(v):
    hi = v.astype(jnp.bfloat16)
    r1 = v - hi.astype(jnp.float32)
    mid = r1.astype(jnp.bfloat16)
    lo = (r1 - mid.astype(jnp.float32)).astype(jnp.bfloat16)
    return hi, mid, lo


def _diff_constants():
    c = np.asarray(_alibi_slopes(H_A), np.float32) * np.float32(LOG2E)
    col = jnp.asarray(c)[:, None] * jnp.arange(TK_A, dtype=jnp.float32)[None, :]
    pieces = jnp.stack(_split3_bf16(col), axis=-1)
    kbias = jnp.zeros((H_A, 2, TK_A, LANES), jnp.bfloat16)
    kbias = kbias.at[:, 0, :, HEAD_DIM:HEAD_DIM + N_BIAS_LANES].set(pieces)
    kbias = kbias.at[:, 1, :, 0:N_BIAS_LANES].set(pieces)
    return jnp.asarray(c), kbias


def _window_bias_c():
    slopes = np.asarray(_alibi_slopes(H_C), np.float64)
    di = np.arange(TQ_W)[:, None]
    dj = np.arange(WIN_C)[None, :]
    tiles = []
    for lead in (0, WINDOW, WIN_C - TQ_W):
        dist = np.abs(lead + di - dj)
        tile = np.where(dist <= WINDOW, -(slopes[:, None, None] * LOG2E) * dist[None], MASKED)
        tiles.append(tile)
    return jnp.asarray(np.stack(tiles).astype(np.float32))


def _window_bias_b(rpb):
    q_rows = TQ_W // GRID_W
    w_rows = WIN_B // GRID_W
    qc = np.arange(GRID_W)[:, None]
    kc = np.arange(GRID_W)[None, :]
    cstart = np.clip(qc - KW // 2, 0, GRID_W - KW)
    col_ok = (kc >= cstart) & (kc < cstart + KW)
    dc = kc - qc + (KW - 1)
    col_sel = (dc[:, :, None] == np.arange(2 * KW - 1)) & col_ok[:, :, None]
    qr = np.arange(q_rows)[:, None]
    wr = np.arange(w_rows)[None, :]
    row_sel, row_oks = [], []
    for lead_rows, rs in ((0, 0 * qr), (MAX_KH // 2, qr), (w_rows - q_rows, 0 * qr + MAX_KH // 2)):
        row_ok = (wr >= rs) & (wr < rs + MAX_KH)
        dr = wr - (lead_rows + qr) + (MAX_KH - 1)
        row_sel.append((dr[:, :, None] == np.arange(2 * MAX_KH - 1)) & row_ok[:, :, None])
        row_oks.append(row_ok)
    row_sel = jnp.asarray(np.stack(row_sel), jnp.float32)
    vals = jnp.einsum("vrwd,hde,cke->vhrcwk", row_sel, rpb.astype(jnp.float32) * LOG2E,
                      jnp.asarray(col_sel, jnp.float32), precision=lax.Precision.HIGHEST)
    ok = np.stack(row_oks)[:, None, :, None, :, None] & col_ok[None, None, None, :, None, :]
    return jnp.where(jnp.asarray(ok), vals, MASKED).reshape(3, H_B, TQ_W, WIN_B)


def _layer_weights(l, w_in, norm_g, qk_gain_a, qk_gain_b, qk_gain_c):
    bf16, f32 = jnp.bfloat16, jnp.float32
    w = w_in[l]
    sizes = (W_A,) * 4 + (W_B,) * 4 + (W_C, KV_C * HEAD_DIM, KV_C * HEAD_DIM, W_C) + (D_MODEL,) * 3
    offs = np.concatenate([[0], np.cumsum(sizes)])
    names = ("qa", "ka", "va", "za", "qb", "kb", "vb", "zb", "qc", "kc", "vc", "zc", "ga", "gb", "gc")
    cols = {n: w[:, offs[i]:offs[i + 1]] for i, n in enumerate(names)}
    dup = lambda a: jnp.concatenate([a[:, :HEAD_DIM], a[:, :HEAD_DIM], a[:, HEAD_DIM:], a[:, HEAD_DIM:]], axis=1)
    cols["kc"] = dup(cols["kc"])
    cols["vc"] = dup(cols["vc"])
    wtok = jnp.concatenate([cols[n] for n, _ in _TOK_SIZES], axis=1).astype(bf16)
    wfeat = jnp.concatenate([cols["qa"], cols["va"]], axis=1).T.astype(bf16)
    qscale = QK_SCALE * LOG2E
    tile8 = lambda g: jnp.tile(g.astype(f32), W_A // HEAD_DIM)
    gains = jnp.stack([tile8(qk_gain_a[l, 1]), tile8(qk_gain_b[l, 0]) * qscale, tile8(qk_gain_b[l, 1]),
                       tile8(qk_gain_c[l, 0]) * qscale, tile8(qk_gain_c[l, 1]),
                       jnp.zeros((W_A,), f32), jnp.zeros((W_A,), f32), jnp.zeros((W_A,), f32)])
    gfeat = (qk_gain_a[l, 0].astype(f32) * qscale)[:, None]
    ng = norm_g[l].astype(f32)[None, :]
    return ng, wtok, wfeat, gains, gfeat


def _trunk(x, layers, consts):
    cvec, kbias, gmat, bias_c, no_sink = consts
    for lw in layers:
        (ng, wtok, wfeat, gains, gfeat, lam, g2, bias_b, sink2, wa, wb, wc, wo) = lw
        (qat, ka, vat, za, qb, kb, vb, zb, qc, kc, vc, zc, g) = _inproj(x, ng, wtok, wfeat, gmat, gains, gfeat)
        oa = _diff_attn(cvec, lam, qat, ka, vat, kbias, g2)
        ob = _win_attn(no_sink, qb, kb, vb, bias_b, win=WIN_B, lead=(MAX_KH // 2) * GRID_W, kv_group=1)
        oc = _win_attn(sink2, qc, kc, vc, bias_c, win=WIN_C, lead=WINDOW, kv_group=2)
        x = _merge(x, oa, ob, oc, za, zb, zc, g, wa, wb, wc, wo)
    return x


def kernel(x_prompt, x_sample, norm_g, w_in, qk_gain_a, lambda_a, subln_g_a, qk_gain_b, rpb_b,
           qk_gain_c, sink_c, w_proj_a, w_proj_b, w_proj_c, w_out):
    bf16, f32 = jnp.bfloat16, jnp.float32
    for x in (x_prompt, x_sample):
        T = x.shape[1]
        assert T % (STEPS_A * TK_A) == 0 and T >= 3 * TQ_W and T >= WIN_B and x.shape[2] == D_MODEL
    cvec, kbias = _diff_constants()
    gidx = np.arange(NORM_TILE) // HEAD_DIM
    gmat = jnp.asarray((gidx[:, None] == gidx[None, :]).astype(np.float32) / HEAD_DIM, bf16)
    consts = (cvec, kbias, gmat, _window_bias_c(), jnp.full((H_B,), MASKED, f32))
    layers = []
    for l in range(DEPTH):
        lam_init = 0.8 - 0.6 * math.exp(-0.3 * l)
        lv = lambda_a[l].astype(f32)
        lam = (jnp.exp(jnp.sum(lv[0] * lv[1])) - jnp.exp(jnp.sum(lv[2] * lv[3])) + lam_init).reshape(1)
        g2 = (subln_g_a[l].astype(f32) * (1.0 - lam_init))[:, None]
        layers.append(_layer_weights(l, w_in, norm_g, qk_gain_a, qk_gain_b, qk_gain_c)
                      + (lam, g2, _window_bias_b(rpb_b[l]), sink_c[l].astype(f32) * LOG2E,
                         w_proj_a[l].astype(bf16), w_proj_b[l].astype(bf16), w_proj_c[l].astype(bf16),
                         w_out[l].astype(bf16)))
    return (_trunk(x_prompt, layers, consts), _trunk(x_sample, layers, consts))
```

```python
import functools
import math

import numpy as np
import jax
import jax.numpy as jnp
from jax import lax
from jax.experimental import pallas as pl
from jax.experimental.pallas import tpu as pltpu

D_MODEL = 1024
DEPTH = 2
HEAD_DIM = 64
EPS = 1e-6
H_A = 4
W_A = H_A * 2 * HEAD_DIM
H_B = 8
W_B = H_B * HEAD_DIM
GRID_W = 64
MAX_KH = 8
KW = 16
H_C = 8
KV_C = 2
W_C = H_C * HEAD_DIM
WINDOW = 128

LOG2E = math.log2(math.e)
QK_SCALE = HEAD_DIM ** -0.5
MASKED = -1e30

LANES = 128
TM = 512
NORM_TILE = 256
TQ_A = 512
TK_A = TM
STEPS_A = 8
TQ_W = 256
QBLOCKS_W = 4
WIN_B = (MAX_KH + 4) * GRID_W
WIN_C = TQ_W + 2 * WINDOW
N_BIAS_LANES = 3
BF16_ROWS = 16
VT_ROWS = 2 * HEAD_DIM + BF16_ROWS
VMEM_LIMIT_BYTES = 56 * 1024 * 1024

def _offsets(sizes):
    offs, o = {}, 0
    for name, size in sizes:
        offs[name] = (o, o + size)
        o += size
    return offs


_TOK_SIZES = (("ka", W_A), ("za", W_A), ("kb", W_B), ("zb", W_B), ("kc", 4 * HEAD_DIM), ("zc", W_C),
              ("ga", D_MODEL), ("gb", D_MODEL), ("gc", D_MODEL))
_FEAT_SIZES = (("qa", W_A), ("va", W_A), ("qb", W_B), ("vb", W_B), ("qc", W_C), ("vc", 4 * HEAD_DIM))
_TOK_OFF = _offsets(_TOK_SIZES)
_FEAT_OFF = _offsets(_FEAT_SIZES)


def _alibi_slopes(n):
    return [2.0 ** (-8.0 * (i + 1) / n) for i in range(n)]


def _params(**kw):
    return pltpu.CompilerParams(vmem_limit_bytes=VMEM_LIMIT_BYTES, **kw)


def _resident(block_shape, index_map):
    return pl.BlockSpec(block_shape, index_map, pipeline_mode=pl.Buffered(1))


def _inproj_kernel(x_ref, ng_ref, wtok_ref, wfeat_ref, gmat_ref, gains_ref, gfeat_ref,
                   qat_ref, ka_ref, vat_ref, za_ref, qbt_ref, kb_ref, vbt_ref, zb_ref,
                   qct_ref, kc_ref, vct_ref, zc_ref, g_ref):
    bf16, f32 = jnp.bfloat16, jnp.float32
    x = x_ref[0]
    ms = jnp.mean(x * x, axis=-1, keepdims=True)
    h = (x * lax.rsqrt(ms + EPS) * ng_ref[...]).astype(bf16)

    def proj(name):
        c0, c1 = _TOK_OFF[name]
        return jnp.dot(h, wtok_ref[:, c0:c1], preferred_element_type=f32)

    def normed(name, gain_row):
        p = proj(name)
        w = p.shape[-1]
        pp = (p * p).astype(bf16)
        ss = jnp.concatenate([jnp.dot(pp[:, c0:c0 + NORM_TILE], gmat_ref[...], preferred_element_type=f32)
                              for c0 in range(0, w, NORM_TILE)], axis=1)
        return (p * lax.rsqrt(ss + EPS) * gains_ref[gain_row:gain_row + 1, :w]).astype(bf16)

    ka_ref[0] = normed("ka", 0)
    za_ref[0] = proj("za").astype(bf16)
    kb_ref[0] = normed("kb", 1)
    zb_ref[0] = proj("zb").astype(bf16)
    kc_ref[0] = normed("kc", 2)
    zc_ref[0] = proj("zc").astype(bf16)
    for i, name in enumerate(("ga", "gb", "gc")):
        g_ref[0, :, i * D_MODEL:(i + 1) * D_MODEL] = proj(name).astype(bf16)

    def proj_t(name):
        r0, r1 = _FEAT_OFF[name]
        return lax.dot_general(wfeat_ref[r0:r1, :], h, (((1,), (1,)), ((), ())), preferred_element_type=f32)

    def put_q(name, gain_idx, out_ref):
        pf = proj_t(name)
        for g in range(pf.shape[0] // HEAD_DIM):
            blk = pf[g * HEAD_DIM:(g + 1) * HEAD_DIM]
            r = lax.rsqrt(jnp.mean(blk * blk, axis=0, keepdims=True) + EPS)
            out_ref[0, g * HEAD_DIM:(g + 1) * HEAD_DIM, :] = (blk * r * gfeat_ref[gain_idx]).astype(bf16)

    put_q("qa", 0, qat_ref)
    put_q("qb", 1, qbt_ref)
    put_q("qc", 2, qct_ref)

    pad_row = lax.broadcasted_iota(jnp.int32, (BF16_ROWS, TM), 0)
    ones_row = jnp.where(pad_row == 0, 1.0, 0.0).astype(bf16)
    nf = 2 * HEAD_DIM
    pf = proj_t("va")
    for hh in range(H_A):
        vat_ref[0, hh, 0, :nf, :] = pf[hh * nf:(hh + 1) * nf].astype(bf16)
        vat_ref[0, hh, 0, nf:, :] = ones_row
    for name, out_ref in (("vb", vbt_ref), ("vc", vct_ref)):
        pf = proj_t(name)
        for p in range(pf.shape[0] // nf):
            for s in range(TM // LANES):
                out_ref[0, p, s, :nf, :] = pf[p * nf:(p + 1) * nf, s * LANES:(s + 1) * LANES].astype(bf16)
                out_ref[0, p, s, nf:, :] = ones_row[:, :LANES]


def _inproj(x, ng, wtok, wfeat, gmat, gains, gfeat):
    B, T, _ = x.shape
    nt = T // TM
    bf16 = jnp.bfloat16
    tok = lambda w: pl.BlockSpec((1, TM, w), lambda b, t: (b, t, 0))
    feat = lambda w: pl.BlockSpec((1, w, TM), lambda b, t: (b, 0, t))
    slabs = TM // LANES
    vt_slabs = lambda n: pl.BlockSpec((1, n, slabs, VT_ROWS, LANES), lambda b, t: (b, 0, t, 0, 0))
    const = lambda a: _resident(a.shape, lambda b, t: (0,) * a.ndim)
    out_shape = (
        jax.ShapeDtypeStruct((B, W_A, T), bf16),
        jax.ShapeDtypeStruct((B, T, W_A), bf16),
        jax.ShapeDtypeStruct((B, H_A, nt, VT_ROWS, TM), bf16),
        jax.ShapeDtypeStruct((B, T, W_A), bf16),
        jax.ShapeDtypeStruct((B, W_B, T), bf16),
        jax.ShapeDtypeStruct((B, T, W_B), bf16),
        jax.ShapeDtypeStruct((B, H_B // 2, T // LANES, VT_ROWS, LANES), bf16),
        jax.ShapeDtypeStruct((B, T, W_B), bf16),
        jax.ShapeDtypeStruct((B, W_C, T), bf16),
        jax.ShapeDtypeStruct((B, T, 4 * HEAD_DIM), bf16),
        jax.ShapeDtypeStruct((B, KV_C, T // LANES, VT_ROWS, LANES), bf16),
        jax.ShapeDtypeStruct((B, T, W_C), bf16),
        jax.ShapeDtypeStruct((B, T, 3 * D_MODEL), bf16),
    )
    out_specs = (
        feat(W_A),
        tok(W_A),
        pl.BlockSpec((1, H_A, 1, VT_ROWS, TM), lambda b, t: (b, 0, t, 0, 0)),
        tok(W_A), feat(W_B), tok(W_B), vt_slabs(H_B // 2), tok(W_B), feat(W_C),
        tok(4 * HEAD_DIM), vt_slabs(KV_C), tok(W_C), tok(3 * D_MODEL),
    )
    return pl.pallas_call(
        _inproj_kernel,
        grid=(B, nt),
        in_specs=[pl.BlockSpec((1, TM, D_MODEL), lambda b, t: (b, t, 0)),
                  const(ng), const(wtok), const(wfeat), const(gmat), const(gains), const(gfeat)],
        out_specs=out_specs,
        out_shape=out_shape,
        compiler_params=_params(dimension_semantics=("parallel", "parallel")),
        name="inproj",
    )(x, ng, wtok, wfeat, gmat, gains, gfeat)


def _steps_per_iteration(nk):
    return STEPS_A if nk >= 2 * STEPS_A else STEPS_A // 2


def _diff_attn_kernel(c_ref, lam_ref, trips_ref, qt_ref, k_ref, vt_ref, kbias_ref, g2_ref, o_ref,
                      kp_ref, qv_ref, sa_ref, sb_ref, ta_ref, tb_ref, m_ref, acc_ref, *, seq):
    bf16, f32 = jnp.bfloat16, jnp.float32
    h = pl.program_id(1)
    qi = pl.program_id(2)
    nk = seq // TK_A
    n_steps = _steps_per_iteration(nk)
    c = c_ref[h]

    @pl.when(qi == 0)
    def _():
        lane = lax.broadcasted_iota(jnp.int32, (TK_A, LANES), 1)
        kb0 = kbias_ref[0, 0].astype(f32)
        kb1 = kbias_ref[0, 1].astype(f32)

        def body(j, carry):
            rows = pl.ds(pl.multiple_of(j * TK_A, TK_A), TK_A)
            kblk = k_ref[0, rows, :].astype(f32)
            kp_ref[0, rows, :] = jnp.where(lane < HEAD_DIM, kblk, kb0).astype(bf16)
            kp_ref[1, rows, :] = jnp.where(lane >= HEAD_DIM, kblk, kb1).astype(bf16)
            return carry

        lax.fori_loop(0, nk, body, 0)

    row = lax.broadcasted_iota(jnp.int32, (2 * HEAD_DIM, TQ_A), 0)
    qblk = qt_ref[0].astype(f32)
    for sub in range(2):
        own = (row < HEAD_DIM) if sub == 0 else (row >= HEAD_DIM)
        b0 = HEAD_DIM if sub == 0 else 0
        brow = (row >= b0) & (row < b0 + N_BIAS_LANES)
        base = jnp.where(own, qblk, 0.0)
        for var, sigma in enumerate((1.0, -1.0, 0.0)):
            qv_ref[3 * sub + var] = jnp.where(brow, sigma, base).astype(bf16)

    m_ref[...] = jnp.full(m_ref.shape, MASKED, f32)
    acc_ref[...] = jnp.zeros(acc_ref.shape, f32)

    i0 = qi * TQ_A
    jd = i0 // TK_A
    di = lax.broadcasted_iota(jnp.int32, (1, TQ_A), 1)

    def update(sub, t, tmax, rvec, vt):
        m_old = m_ref[sub]
        m_new = jnp.maximum(m_old, tmax + rvec)
        alpha = jnp.exp2(m_old - m_new)
        p = jnp.exp2(t - (m_new - rvec))
        acc_ref[sub] = alpha * acc_ref[sub] + jnp.dot(vt, p.astype(bf16), preferred_element_type=f32)
        m_ref[sub] = m_new

    def put_scores(s_ref, t_ref, sub, t):
        s_ref[sub] = t
        t_ref[sub] = jnp.max(t, axis=0, keepdims=True)

    def block_of(step):
        jj = step - 1
        j = jnp.where(step == 0, jd, jj + (jj >= jd).astype(jnp.int32))
        return j, j > jd

    def produce(bufs, step):
        s_ref, t_ref = bufs
        j, after = block_of(step)
        var = after.astype(jnp.int32)
        rows = pl.ds(pl.multiple_of(j * TK_A, TK_A), TK_A)
        for sub in range(2):
            put_scores(s_ref, t_ref, sub,
                       jnp.dot(kp_ref[sub, rows, :], qv_ref[3 * sub + var], preferred_element_type=f32))

    def consume(bufs, step):
        s_ref, t_ref = bufs
        j, after = block_of(step)
        sgn_c = jnp.where(j == jd, 0.0, jnp.where(after, c, -c)).astype(f32)
        rvec = sgn_c * (i0 - j * TK_A + di).astype(f32)
        vt = vt_ref[0, 0, j]
        for sub in range(2):
            update(sub, s_ref[sub], t_ref[sub], rvec, vt)

    buf_a = (sa_ref, ta_ref)
    buf_b = (sb_ref, tb_ref)

    rows_d = pl.ds(pl.multiple_of(jd * TK_A, TK_A), TK_A)
    off = i0 - jd * TK_A
    dj = lax.broadcasted_iota(jnp.int32, (TK_A, TQ_A), 0)
    dq = lax.broadcasted_iota(jnp.int32, (TK_A, TQ_A), 1)
    dbias = -c * jnp.abs(off + dq - dj).astype(f32)
    for sub in range(2):
        put_scores(sa_ref, ta_ref, sub,
                   jnp.dot(kp_ref[sub, rows_d, :], qv_ref[3 * sub + 2], preferred_element_type=f32) + dbias)

    def steps(s0, produce_last):
        for k in range(0, n_steps, 2):
            produce(buf_b, s0 + k + 1)
            consume(buf_a, s0 + k)
            if produce_last or k + 2 < n_steps:
                produce(buf_a, s0 + k + 2)
            consume(buf_b, s0 + k + 1)

    def body(i, carry):
        steps(n_steps * i, True)
        return carry

    lax.fori_loop(0, trips_ref[0], body, 0)
    steps(nk - n_steps, False)

    nf = 2 * HEAD_DIM
    o = (acc_ref[0, :nf, :] / acc_ref[0, nf:nf + 1, :]
         - lam_ref[0] * (acc_ref[1, :nf, :] / acc_ref[1, nf:nf + 1, :]))
    y = o * lax.rsqrt(jnp.mean(o * o, axis=0, keepdims=True) + EPS) * g2_ref[...]
    o_ref[0] = y.T.astype(o_ref.dtype)


def _diff_attn(cvec, lam, qat, ka, vat, kbias, g2):
    B, _, T = qat.shape
    nq = T // TQ_A
    nkb = T // TK_A
    smem = pl.BlockSpec(memory_space=pltpu.SMEM)
    trips = jnp.full((1,), nkb // _steps_per_iteration(nkb) - 1, jnp.int32)
    return pl.pallas_call(
        functools.partial(_diff_attn_kernel, seq=T),
        grid=(B, H_A, nq),
        in_specs=[
            smem, smem, smem,
            pl.BlockSpec((1, 2 * HEAD_DIM, TQ_A), lambda b, h, q: (b, h, q)),
            pl.BlockSpec((1, T, 2 * HEAD_DIM), lambda b, h, q: (b, 0, h)),
            pl.BlockSpec((1, 1, nkb, VT_ROWS, TK_A), lambda b, h, q: (b, h, 0, 0, 0)),
            pl.BlockSpec((1, 2, TK_A, LANES), lambda b, h, q: (h, 0, 0, 0)),
            pl.BlockSpec((2 * HEAD_DIM, 1), lambda b, h, q: (0, 0)),
        ],
        out_specs=pl.BlockSpec((1, TQ_A, 2 * HEAD_DIM), lambda b, h, q: (b, q, h)),
        out_shape=jax.ShapeDtypeStruct((B, T, W_A), jnp.bfloat16),
        scratch_shapes=[
            pltpu.VMEM((2, T, LANES), jnp.bfloat16),
            pltpu.VMEM((6, 2 * HEAD_DIM, TQ_A), jnp.bfloat16),
            pltpu.VMEM((2, TK_A, TQ_A), jnp.float32),
            pltpu.VMEM((2, TK_A, TQ_A), jnp.float32),
            pltpu.VMEM((2, 1, TQ_A), jnp.float32),
            pltpu.VMEM((2, 1, TQ_A), jnp.float32),
            pltpu.VMEM((2, 1, TQ_A), jnp.float32),
            pltpu.VMEM((2, VT_ROWS, TQ_A), jnp.float32),
        ],
        compiler_params=_params(dimension_semantics=("parallel", "parallel", "arbitrary")),
        name="diff_attn",
    )(cvec, lam, trips, qat, ka, vat, kbias, g2)


def _win_attn_kernel(sink_ref, qt_ref, k_ref, vt_ref, bias_ref, o_ref, *, seq, win, lead):
    bf16, f32 = jnp.bfloat16, jnp.float32
    pair = pl.program_id(1)
    step = pl.program_id(2)
    nq = seq // TQ_W
    nf = 2 * HEAD_DIM
    row = lax.broadcasted_iota(jnp.int32, (nf, TQ_W), 0)
    sinks = [sink_ref[2 * pair + hf] for hf in range(2)]

    jobs = [(n, hf) for n in range(QBLOCKS_W) for hf in range(2)]
    kws, vts, variants, qts = [], [], [], []
    for n in range(QBLOCKS_W):
        qi = step * QBLOCKS_W + n
        ws = pl.multiple_of(jnp.clip(qi * TQ_W - lead, 0, seq - win), LANES)
        variants.append(jnp.where(qi == 0, 0, jnp.where(qi == nq - 1, 2, 1)))
        kws.append(k_ref[0, pl.ds(ws, win), :])
        slab = ws // LANES
        vts.append(jnp.concatenate([vt_ref[0, 0, slab + c] for c in range(win // LANES)], axis=1))
        qts.append(qt_ref[0, :, n * TQ_W:(n + 1) * TQ_W].astype(f32))
    qms = [jnp.where((row < HEAD_DIM) if hf == 0 else (row >= HEAD_DIM), qts[n], 0.0).astype(bf16)
           for n, hf in jobs]
    sts = [jnp.dot(kws[n], qms[j], preferred_element_type=f32) + bias_ref[variants[n], hf]
           for j, (n, hf) in enumerate(jobs)]
    ms = [jnp.maximum(jnp.max(sts[j], axis=0, keepdims=True), sinks[hf]) for j, (n, hf) in enumerate(jobs)]
    pts = [jnp.exp2(sts[j] - ms[j]).astype(bf16) for j in range(len(jobs))]
    accs = [jnp.dot(vts[n], pts[j], preferred_element_type=f32) for j, (n, hf) in enumerate(jobs)]
    ots = [accs[j][hf * HEAD_DIM:(hf + 1) * HEAD_DIM] / (accs[j][nf:nf + 1] + jnp.exp2(sinks[hf] - ms[j]))
           for j, (n, hf) in enumerate(jobs)]
    for n in range(QBLOCKS_W):
        ot = jnp.concatenate([ots[2 * n], ots[2 * n + 1]], axis=0)
        o_ref[0, n * TQ_W:(n + 1) * TQ_W, :] = ot.T.astype(o_ref.dtype)


def _win_attn(sink2, qt, k, vt, bias, *, win, lead, kv_group):
    B, width, T = qt.shape
    n_pairs = width // LANES
    tq_step = QBLOCKS_W * TQ_W
    return pl.pallas_call(
        functools.partial(_win_attn_kernel, seq=T, win=win, lead=lead),
        grid=(B, n_pairs, T // tq_step),
        in_specs=[
            pl.BlockSpec(memory_space=pltpu.SMEM),
            pl.BlockSpec((1, LANES, tq_step), lambda b, p, s: (b, p, s)),
            pl.BlockSpec((1, T, LANES), lambda b, p, s: (b, 0, p // kv_group)),
            pl.BlockSpec((1, 1, T // LANES, VT_ROWS, LANES), lambda b, p, s: (b, p // kv_group, 0, 0, 0)),
            pl.BlockSpec((3, 2, win, TQ_W), lambda b, p, s: (0, p, 0, 0)),
        ],
        out_specs=pl.BlockSpec((1, tq_step, LANES), lambda b, p, s: (b, s, p)),
        out_shape=jax.ShapeDtypeStruct((B, T, width), jnp.bfloat16),
        compiler_params=_params(dimension_semantics=("parallel", "parallel", "arbitrary")),
        name="win_attn",
    )(sink2, qt, k, vt, bias)


def _merge_kernel(x_ref, oa_ref, ob_ref, oc_ref, za_ref, zb_ref, zc_ref, g_ref,
                  wa_ref, wb_ref, wc_ref, wo_ref, y_ref):
    bf16, f32 = jnp.bfloat16, jnp.float32

    def branch(o_ref, z_ref, w_ref, i):
        z = z_ref[0].astype(f32)
        u = o_ref[0].astype(f32) * (z * jax.nn.sigmoid(z))
        y = jnp.dot(u.astype(bf16), w_ref[...], preferred_element_type=f32)
        gate = g_ref[0, :, i * D_MODEL:(i + 1) * D_MODEL].astype(f32)
        return jax.nn.sigmoid(gate) * y

    m = branch(oa_ref, za_ref, wa_ref, 0) + branch(ob_ref, zb_ref, wb_ref, 1) + branch(oc_ref, zc_ref, wc_ref, 2)
    y_ref[0] = x_ref[0] + jnp.dot(m.astype(bf16), wo_ref[...], preferred_element_type=f32)


def _merge(x, oa, ob, oc, za, zb, zc, g, wa, wb, wc, wo):
    B, T, _ = x.shape
    tok = lambda w: pl.BlockSpec((1, TM, w), lambda b, t: (b, t, 0))
    const2 = lambda a: _resident(a.shape, lambda b, t: (0, 0))
    return pl.pallas_call(
        _merge_kernel,
        grid=(B, T // TM),
        in_specs=[tok(D_MODEL), tok(W_A), tok(W_B), tok(W_C), tok(W_A), tok(W_B), tok(W_C), tok(3 * D_MODEL),
                  const2(wa), const2(wb), const2(wc), const2(wo)],
        out_specs=tok(D_MODEL),
        out_shape=jax.ShapeDtypeStruct(x.shape, x.dtype),
        compiler_params=_params(dimension_semantics=("parallel", "parallel")),
        name="merge",
    )(x, oa, ob, oc, za, zb, zc, g, wa, wb, wc, wo)


def _split3_bf16(v):
    hi = v.astype(jnp.bfloat16)
    r1 = v - hi.astype(jnp.float32)
    mid = r1.astype(jnp.bfloat16)
    lo = (r1 - mid.astype(jnp.float32)).astype(jnp.bfloat16)
    return hi, mid, lo


def _diff_constants():
    c = np.asarray(_alibi_slopes(H_A), np.float32) * np.float32(LOG2E)
    col = jnp.asarray(c)[:, None] * jnp.arange(TK_A, dtype=jnp.float32)[None, :]
    pieces = jnp.stack(_split3_bf16(col), axis=-1)
    kbias = jnp.zeros((H_A, 2, TK_A, LANES), jnp.bfloat16)
    kbias = kbias.at[:, 0, :, HEAD_DIM:HEAD_DIM + N_BIAS_LANES].set(pieces)
    kbias = kbias.at[:, 1, :, 0:N_BIAS_LANES].set(pieces)
    return jnp.asarray(c), kbias


def _window_bias_c():
    slopes = np.asarray(_alibi_slopes(H_C), np.float64)
    di = np.arange(TQ_W)[None, :]
    dj = np.arange(WIN_C)[:, None]
    tiles = []
    for lead in (0, WINDOW, WIN_C - TQ_W):
        dist = np.abs(lead + di - dj)
        tile = np.where(dist <= WINDOW, -(slopes[:, None, None] * LOG2E) * dist[None], MASKED)
        tiles.append(tile)
    return jnp.asarray(np.stack(tiles).astype(np.float32))


def _window_bias_b(rpb):
    q_rows = TQ_W // GRID_W
    w_rows = WIN_B // GRID_W
    qc = np.arange(GRID_W)[:, None]
    kc = np.arange(GRID_W)[None, :]
    cstart = np.clip(qc - KW // 2, 0, GRID_W - KW)
    col_ok = (kc >= cstart) & (kc < cstart + KW)
    dc = kc - qc + (KW - 1)
    col_sel = (dc[:, :, None] == np.arange(2 * KW - 1)) & col_ok[:, :, None]
    qr = np.arange(q_rows)[:, None]
    wr = np.arange(w_rows)[None, :]
    row_sel, row_oks = [], []
    for lead_rows, rs in ((0, 0 * qr), (MAX_KH // 2, qr), (w_rows - q_rows, 0 * qr + MAX_KH // 2)):
        row_ok = (wr >= rs) & (wr < rs + MAX_KH)
        dr = wr - (lead_rows + qr) + (MAX_KH - 1)
        row_sel.append((dr[:, :, None] == np.arange(2 * MAX_KH - 1)) & row_ok[:, :, None])
        row_oks.append(row_ok)
    row_sel = jnp.asarray(np.stack(row_sel), jnp.float32)
    vals = jnp.einsum("vrwd,hde,cke->vhwkrc", row_sel, rpb.astype(jnp.float32) * LOG2E,
                      jnp.asarray(col_sel, jnp.float32), precision=lax.Precision.HIGHEST)
    row_ok_wr = np.stack(row_oks).transpose(0, 2, 1)
    ok = row_ok_wr[:, None, :, None, :, None] & col_ok.T[None, None, None, :, None, :]
    return jnp.where(jnp.asarray(ok), vals, MASKED).reshape(3, H_B, WIN_B, TQ_W)


def _layer_weights(l, w_in, norm_g, qk_gain_a, qk_gain_b, qk_gain_c):
    bf16, f32 = jnp.bfloat16, jnp.float32
    w = w_in[l]
    sizes = (W_A,) * 4 + (W_B,) * 4 + (W_C, KV_C * HEAD_DIM, KV_C * HEAD_DIM, W_C) + (D_MODEL,) * 3
    offs = np.concatenate([[0], np.cumsum(sizes)])
    names = ("qa", "ka", "va", "za", "qb", "kb", "vb", "zb", "qc", "kc", "vc", "zc", "ga", "gb", "gc")
    cols = {n: w[:, offs[i]:offs[i + 1]] for i, n in enumerate(names)}
    dup = lambda a: jnp.concatenate([a[:, :HEAD_DIM], a[:, :HEAD_DIM], a[:, HEAD_DIM:], a[:, HEAD_DIM:]], axis=1)
    cols["kc"] = dup(cols["kc"])
    cols["vc"] = dup(cols["vc"])
    wtok = jnp.concatenate([cols[n] for n, _ in _TOK_SIZES], axis=1).astype(bf16)
    wfeat = jnp.concatenate([cols[n] for n, _ in _FEAT_SIZES], axis=1).T.astype(bf16)
    qscale = QK_SCALE * LOG2E
    tile8 = lambda g: jnp.tile(g.astype(f32), W_A // HEAD_DIM)
    zero = jnp.zeros((W_A,), f32)
    gains = jnp.stack([tile8(qk_gain_a[l, 1]), tile8(qk_gain_b[l, 1]), tile8(qk_gain_c[l, 1])] + [zero] * 5)
    gfeat = jnp.stack([qk_gain_a[l, 0], qk_gain_b[l, 0], qk_gain_c[l, 0]]).astype(f32)[:, :, None] * qscale
    ng = norm_g[l].astype(f32)[None, :]
    return ng, wtok, wfeat, gains, gfeat


def _trunk(x, layers, consts):
    cvec, kbias, gmat, bias_c, no_sink = consts
    for lw in layers:
        (ng, wtok, wfeat, gains, gfeat, lam, g2, bias_b, sink2, wa, wb, wc, wo) = lw
        (qat, ka, vat, za, qbt, kb, vbt, zb, qct, kc, vct, zc, g) = _inproj(x, ng, wtok, wfeat, gmat, gains, gfeat)
        oa = _diff_attn(cvec, lam, qat, ka, vat, kbias, g2)
        ob = _win_attn(no_sink, qbt, kb, vbt, bias_b, win=WIN_B, lead=(MAX_KH // 2) * GRID_W, kv_group=1)
        oc = _win_attn(sink2, qct, kc, vct, bias_c, win=WIN_C, lead=WINDOW, kv_group=2)
        x = _merge(x, oa, ob, oc, za, zb, zc, g, wa, wb, wc, wo)
    return x


def kernel(x_prompt, x_sample, norm_g, w_in, qk_gain_a, lambda_a, subln_g_a, qk_gain_b, rpb_b,
           qk_gain_c, sink_c, w_proj_a, w_proj_b, w_proj_c, w_out):
    bf16, f32 = jnp.bfloat16, jnp.float32
    for x in (x_prompt, x_sample):
        T = x.shape[1]
        assert T % (STEPS_A * TK_A) == 0 and T >= 3 * TQ_W and T >= WIN_B and x.shape[2] == D_MODEL
    cvec, kbias = _diff_constants()
    gidx = np.arange(NORM_TILE) // HEAD_DIM
    gmat = jnp.asarray((gidx[:, None] == gidx[None, :]).astype(np.float32) / HEAD_DIM, bf16)
    consts = (cvec, kbias, gmat, _window_bias_c(), jnp.full((H_B,), MASKED, f32))
    layers = []
    for l in range(DEPTH):
        lam_init = 0.8 - 0.6 * math.exp(-0.3 * l)
        lv = lambda_a[l].astype(f32)
        lam = (jnp.exp(jnp.sum(lv[0] * lv[1])) - jnp.exp(jnp.sum(lv[2] * lv[3])) + lam_init).reshape(1)
        g2 = (subln_g_a[l].astype(f32) * (1.0 - lam_init))[:, None]
        layers.append(_layer_weights(l, w_in, norm_g, qk_gain_a, qk_gain_b, qk_gain_c)
                      + (lam, g2, _window_bias_b(rpb_b[l]), sink_c[l].astype(f32) * LOG2E,
                         w_proj_a[l].astype(bf16), w_proj_b[l].astype(bf16), w_proj_c[l].astype(bf16),
                         w_out[l].astype(bf16)))
    return (_trunk(x_prompt, layers, consts), _trunk(x_sample, layers, consts))
```

```python
import functools
import math

import numpy as np
import jax
import jax.numpy as jnp
from jax import lax
from jax.experimental import pallas as pl
from jax.experimental.pallas import tpu as pltpu

D_MODEL = 1024
DEPTH = 2
HEAD_DIM = 64
EPS = 1e-6
H_A = 4
W_A = H_A * 2 * HEAD_DIM
H_B = 8
W_B = H_B * HEAD_DIM
GRID_W = 64
MAX_KH = 8
KW = 16
H_C = 8
KV_C = 2
W_C = H_C * HEAD_DIM
WINDOW = 128

LOG2E = math.log2(math.e)
QK_SCALE = HEAD_DIM ** -0.5
MASKED = -1e30

LANES = 128
TM = 512
NORM_TILE = 256
TQ_A = TM
TK_A = TM
STEPS_A = 8
TQ_W = 256
QBLOCKS_W = 8
WIN_B = (MAX_KH + 4) * GRID_W
WIN_C = TQ_W + 2 * WINDOW
N_BIAS_LANES = 3
BF16_ROWS = 16
VT_ROWS = 2 * HEAD_DIM + BF16_ROWS
VMEM_LIMIT_BYTES = 56 * 1024 * 1024


def _offsets(sizes):
    offs, o = {}, 0
    for name, size in sizes:
        offs[name] = (o, o + size)
        o += size
    return offs


_TOK_SIZES = (("ka", W_A), ("za", W_A), ("kb", W_B), ("zb", W_B), ("kc", 4 * HEAD_DIM), ("zc", W_C),
              ("ga", D_MODEL), ("gb", D_MODEL), ("gc", D_MODEL))
_FEAT_SIZES = (("qa", W_A), ("va", W_A), ("qb", W_B), ("vb", W_B), ("qc", W_C), ("vc", 4 * HEAD_DIM))
_TOK_OFF = _offsets(_TOK_SIZES)
_FEAT_OFF = _offsets(_FEAT_SIZES)


def _alibi_slopes(n):
    return [2.0 ** (-8.0 * (i + 1) / n) for i in range(n)]


def _params(**kw):
    return pltpu.CompilerParams(vmem_limit_bytes=VMEM_LIMIT_BYTES, **kw)


def _resident(block_shape, index_map):
    return pl.BlockSpec(block_shape, index_map, pipeline_mode=pl.Buffered(1))


def _inproj_kernel(x_ref, ng_ref, wtok_ref, wfeat_ref, gmat_ref, gains_ref, gfeat_ref,
                   qat_ref, ka_ref, vat_ref, za_ref, qbt_ref, kb_ref, vbt_ref, zb_ref,
                   qct_ref, kc_ref, vct_ref, zc_ref, g_ref):
    bf16, f32 = jnp.bfloat16, jnp.float32
    x = x_ref[0]
    ms = jnp.mean(x * x, axis=-1, keepdims=True)
    h = (x * lax.rsqrt(ms + EPS) * ng_ref[...]).astype(bf16)

    def proj(name):
        c0, c1 = _TOK_OFF[name]
        return jnp.dot(h, wtok_ref[:, c0:c1], preferred_element_type=f32)

    def normed(name, gain_row):
        p = proj(name)
        w = p.shape[-1]
        pp = (p * p).astype(bf16)
        ss = jnp.concatenate([jnp.dot(pp[:, c0:c0 + NORM_TILE], gmat_ref[...], preferred_element_type=f32)
                              for c0 in range(0, w, NORM_TILE)], axis=1)
        return (p * lax.rsqrt(ss + EPS) * gains_ref[gain_row:gain_row + 1, :w]).astype(bf16)

    ka_ref[0] = normed("ka", 0)
    za_ref[0] = proj("za").astype(bf16)
    kb_ref[0] = normed("kb", 1)
    zb_ref[0] = proj("zb").astype(bf16)
    kc_ref[0] = normed("kc", 2)
    zc_ref[0] = proj("zc").astype(bf16)
    for i, name in enumerate(("ga", "gb", "gc")):
        g_ref[0, :, i * D_MODEL:(i + 1) * D_MODEL] = proj(name).astype(bf16)

    def proj_t(name):
        r0, r1 = _FEAT_OFF[name]
        return lax.dot_general(wfeat_ref[r0:r1, :], h, (((1,), (1,)), ((), ())), preferred_element_type=f32)

    def put_q(name, gain_idx, out_ref):
        pf = proj_t(name)
        for g in range(pf.shape[0] // HEAD_DIM):
            blk = pf[g * HEAD_DIM:(g + 1) * HEAD_DIM]
            r = lax.rsqrt(jnp.mean(blk * blk, axis=0, keepdims=True) + EPS)
            out_ref[0, g * HEAD_DIM:(g + 1) * HEAD_DIM, :] = (blk * r * gfeat_ref[gain_idx]).astype(bf16)

    put_q("qa", 0, qat_ref)
    put_q("qb", 1, qbt_ref)
    put_q("qc", 2, qct_ref)

    pad_row = lax.broadcasted_iota(jnp.int32, (BF16_ROWS, TM), 0)
    ones_row = jnp.where(pad_row == 0, 1.0, 0.0).astype(bf16)
    nf = 2 * HEAD_DIM
    pf = proj_t("va")
    for hh in range(H_A):
        vat_ref[0, hh, 0, :nf, :] = pf[hh * nf:(hh + 1) * nf].astype(bf16)
        vat_ref[0, hh, 0, nf:, :] = ones_row
    for name, out_ref in (("vb", vbt_ref), ("vc", vct_ref)):
        pf = proj_t(name)
        for p in range(pf.shape[0] // nf):
            for s in range(TM // LANES):
                out_ref[0, p, s, :nf, :] = pf[p * nf:(p + 1) * nf, s * LANES:(s + 1) * LANES].astype(bf16)
                out_ref[0, p, s, nf:, :] = ones_row[:, :LANES]


def _inproj(x, ng, wtok, wfeat, gmat, gains, gfeat):
    B, T, _ = x.shape
    nt = T // TM
    bf16 = jnp.bfloat16
    tok = lambda w: pl.BlockSpec((1, TM, w), lambda b, t: (b, t, 0))
    feat = lambda w: pl.BlockSpec((1, w, TM), lambda b, t: (b, 0, t))
    slabs = TM // LANES
    vt_slabs = lambda n: pl.BlockSpec((1, n, slabs, VT_ROWS, LANES), lambda b, t: (b, 0, t, 0, 0))
    const = lambda a: _resident(a.shape, lambda b, t: (0,) * a.ndim)
    out_shape = (
        jax.ShapeDtypeStruct((B, W_A, T), bf16),
        jax.ShapeDtypeStruct((B, T, W_A), bf16),
        jax.ShapeDtypeStruct((B, H_A, nt, VT_ROWS, TM), bf16),
        jax.ShapeDtypeStruct((B, T, W_A), bf16),
        jax.ShapeDtypeStruct((B, W_B, T), bf16),
        jax.ShapeDtypeStruct((B, T, W_B), bf16),
        jax.ShapeDtypeStruct((B, H_B // 2, T // LANES, VT_ROWS, LANES), bf16),
        jax.ShapeDtypeStruct((B, T, W_B), bf16),
        jax.ShapeDtypeStruct((B, W_C, T), bf16),
        jax.ShapeDtypeStruct((B, T, 4 * HEAD_DIM), bf16),
        jax.ShapeDtypeStruct((B, KV_C, T // LANES, VT_ROWS, LANES), bf16),
        jax.ShapeDtypeStruct((B, T, W_C), bf16),
        jax.ShapeDtypeStruct((B, T, 3 * D_MODEL), bf16),
    )
    out_specs = (
        feat(W_A),
        tok(W_A),
        pl.BlockSpec((1, H_A, 1, VT_ROWS, TM), lambda b, t: (b, 0, t, 0, 0)),
        tok(W_A), feat(W_B), tok(W_B), vt_slabs(H_B // 2), tok(W_B), feat(W_C),
        tok(4 * HEAD_DIM), vt_slabs(KV_C), tok(W_C), tok(3 * D_MODEL),
    )
    return pl.pallas_call(
        _inproj_kernel,
        grid=(B, nt),
        in_specs=[pl.BlockSpec((1, TM, D_MODEL), lambda b, t: (b, t, 0)),
                  const(ng), const(wtok), const(wfeat), const(gmat), const(gains), const(gfeat)],
        out_specs=out_specs,
        out_shape=out_shape,
        compiler_params=_params(dimension_semantics=("parallel", "parallel")),
        name="inproj",
    )(x, ng, wtok, wfeat, gmat, gains, gfeat)


def _steps_per_iteration(nk):
    return STEPS_A if nk >= 2 * STEPS_A else STEPS_A // 2


def _diff_attn_kernel(c_ref, lam_ref, trips_ref, qt_ref, k_ref, vt_ref, kbias_ref, dbias_ref, g2_ref, o_ref,
                      kp_ref, qv_ref, sa_ref, sb_ref, ta_ref, tb_ref, m_ref, acc_ref, *, seq):
    bf16, f32 = jnp.bfloat16, jnp.float32
    h = pl.program_id(1)
    qi = pl.program_id(2)
    nk = seq // TK_A
    n_steps = _steps_per_iteration(nk)
    c = c_ref[h]

    @pl.when(qi == 0)
    def _():
        lane = lax.broadcasted_iota(jnp.int32, (TK_A, LANES), 1)
        kb0 = kbias_ref[0, 0].astype(f32)
        kb1 = kbias_ref[0, 1].astype(f32)

        def body(j, carry):
            rows = pl.ds(pl.multiple_of(j * TK_A, TK_A), TK_A)
            kblk = k_ref[0, rows, :].astype(f32)
            kp_ref[0, rows, :] = jnp.where(lane < HEAD_DIM, kblk, kb0).astype(bf16)
            kp_ref[1, rows, :] = jnp.where(lane >= HEAD_DIM, kblk, kb1).astype(bf16)
            return carry

        lax.fori_loop(0, nk, body, 0)

    row = lax.broadcasted_iota(jnp.int32, (2 * HEAD_DIM, TQ_A), 0)
    qblk = qt_ref[0].astype(f32)
    for sub in range(2):
        own = (row < HEAD_DIM) if sub == 0 else (row >= HEAD_DIM)
        b0 = HEAD_DIM if sub == 0 else 0
        brow = (row >= b0) & (row < b0 + N_BIAS_LANES)
        base = jnp.where(own, qblk, 0.0)
        for var, sigma in enumerate((1.0, -1.0, 0.0)):
            qv_ref[3 * sub + var] = jnp.where(brow, sigma, base).astype(bf16)

    m_ref[...] = jnp.full(m_ref.shape, MASKED, f32)
    acc_ref[...] = jnp.zeros(acc_ref.shape, f32)

    i0 = qi * TQ_A
    jd = i0 // TK_A
    di = lax.broadcasted_iota(jnp.int32, (1, TQ_A), 1)

    def update(sub, t, tmax, rvec, vt):
        m_old = m_ref[sub]
        m_new = jnp.maximum(m_old, tmax + rvec)
        alpha = jnp.exp2(m_old - m_new)
        p = jnp.exp2(t - (m_new - rvec))
        acc_ref[sub] = alpha * acc_ref[sub] + jnp.dot(vt, p.astype(bf16), preferred_element_type=f32)
        m_ref[sub] = m_new

    def put_scores(s_ref, t_ref, sub, t):
        s_ref[sub] = t
        t_ref[sub] = jnp.max(t, axis=0, keepdims=True)

    def block_of(step):
        jj = step - 1
        j = jnp.where(step == 0, jd, jj + (jj >= jd).astype(jnp.int32))
        return j, j > jd

    def produce(bufs, step):
        s_ref, t_ref = bufs
        j, after = block_of(step)
        var = after.astype(jnp.int32)
        rows = pl.ds(pl.multiple_of(j * TK_A, TK_A), TK_A)
        for sub in range(2):
            put_scores(s_ref, t_ref, sub,
                       jnp.dot(kp_ref[sub, rows, :], qv_ref[3 * sub + var], preferred_element_type=f32))

    def consume(bufs, step):
        s_ref, t_ref = bufs
        j, after = block_of(step)
        sgn_c = jnp.where(j == jd, 0.0, jnp.where(after, c, -c)).astype(f32)
        rvec = sgn_c * (i0 - j * TK_A + di).astype(f32)
        vt = vt_ref[0, 0, j]
        for sub in range(2):
            update(sub, s_ref[sub], t_ref[sub], rvec, vt)

    buf_a = (sa_ref, ta_ref)
    buf_b = (sb_ref, tb_ref)

    rows_d = pl.ds(pl.multiple_of(jd * TK_A, TK_A), TK_A)
    for sub in range(2):
        put_scores(sa_ref, ta_ref, sub,
                   jnp.dot(kp_ref[sub, rows_d, :], qv_ref[3 * sub + 2], preferred_element_type=f32) + dbias_ref[0])

    def steps(s0, produce_last):
        for k in range(0, n_steps, 2):
            produce(buf_b, s0 + k + 1)
            consume(buf_a, s0 + k)
            if produce_last or k + 2 < n_steps:
                produce(buf_a, s0 + k + 2)
            consume(buf_b, s0 + k + 1)

    def body(i, carry):
        steps(n_steps * i, True)
        return carry

    lax.fori_loop(0, trips_ref[0], body, 0)
    steps(nk - n_steps, False)

    nf = 2 * HEAD_DIM
    o = (acc_ref[0, :nf, :] / acc_ref[0, nf:nf + 1, :]
         - lam_ref[0] * (acc_ref[1, :nf, :] / acc_ref[1, nf:nf + 1, :]))
    y = o * lax.rsqrt(jnp.mean(o * o, axis=0, keepdims=True) + EPS) * g2_ref[...]
    o_ref[0] = y.T.astype(o_ref.dtype)


def _diff_attn(cvec, lam, qat, ka, vat, kbias, dbias, g2):
    B, _, T = qat.shape
    nq = T // TQ_A
    nkb = T // TK_A
    smem = pl.BlockSpec(memory_space=pltpu.SMEM)
    trips = jnp.full((1,), nkb // _steps_per_iteration(nkb) - 1, jnp.int32)
    return pl.pallas_call(
        functools.partial(_diff_attn_kernel, seq=T),
        grid=(B, H_A, nq),
        in_specs=[
            smem, smem, smem,
            pl.BlockSpec((1, 2 * HEAD_DIM, TQ_A), lambda b, h, q: (b, h, q)),
            pl.BlockSpec((1, T, 2 * HEAD_DIM), lambda b, h, q: (b, 0, h)),
            pl.BlockSpec((1, 1, nkb, VT_ROWS, TK_A), lambda b, h, q: (b, h, 0, 0, 0)),
            pl.BlockSpec((1, 2, TK_A, LANES), lambda b, h, q: (h, 0, 0, 0)),
            pl.BlockSpec((1, TK_A, TQ_A), lambda b, h, q: (h, 0, 0)),
            pl.BlockSpec((2 * HEAD_DIM, 1), lambda b, h, q: (0, 0)),
        ],
        out_specs=pl.BlockSpec((1, TQ_A, 2 * HEAD_DIM), lambda b, h, q: (b, q, h)),
        out_shape=jax.ShapeDtypeStruct((B, T, W_A), jnp.bfloat16),
        scratch_shapes=[
            pltpu.VMEM((2, T, LANES), jnp.bfloat16),
            pltpu.VMEM((6, 2 * HEAD_DIM, TQ_A), jnp.bfloat16),
            pltpu.VMEM((2, TK_A, TQ_A), jnp.float32),
            pltpu.VMEM((2, TK_A, TQ_A), jnp.float32),
            pltpu.VMEM((2, 1, TQ_A), jnp.float32),
            pltpu.VMEM((2, 1, TQ_A), jnp.float32),
            pltpu.VMEM((2, 1, TQ_A), jnp.float32),
            pltpu.VMEM((2, VT_ROWS, TQ_A), jnp.float32),
        ],
        compiler_params=_params(dimension_semantics=("parallel", "parallel", "arbitrary")),
        name="diff_attn",
    )(cvec, lam, trips, qat, ka, vat, kbias, dbias, g2)


def _win_attn_kernel(sink_ref, qt_ref, k_ref, vt_ref, bias_ref, o_ref, *, seq, win, lead):
    bf16, f32 = jnp.bfloat16, jnp.float32
    pair = pl.program_id(1)
    step = pl.program_id(2)
    nq = seq // TQ_W
    nf = 2 * HEAD_DIM
    row = lax.broadcasted_iota(jnp.int32, (nf, TQ_W), 0)
    sinks = [sink_ref[2 * pair + hf] for hf in range(2)]

    jobs = [(n, hf) for n in range(QBLOCKS_W) for hf in range(2)]
    kws, vts, variants, qts = [], [], [], []
    for n in range(QBLOCKS_W):
        qi = step * QBLOCKS_W + n
        ws = pl.multiple_of(jnp.clip(qi * TQ_W - lead, 0, seq - win), LANES)
        variants.append(jnp.where(qi == 0, 0, jnp.where(qi == nq - 1, 2, 1)))
        kws.append(k_ref[0, pl.ds(ws, win), :])
        slab = ws // LANES
        vts.append(jnp.concatenate([vt_ref[0, 0, slab + c] for c in range(win // LANES)], axis=1))
        qts.append(qt_ref[0, :, n * TQ_W:(n + 1) * TQ_W].astype(f32))
    qms = [jnp.where((row < HEAD_DIM) if hf == 0 else (row >= HEAD_DIM), qts[n], 0.0).astype(bf16)
           for n, hf in jobs]
    sts = [jnp.dot(kws[n], qms[j], preferred_element_type=f32) + bias_ref[variants[n], hf]
           for j, (n, hf) in enumerate(jobs)]
    ms = [jnp.maximum(jnp.max(sts[j], axis=0, keepdims=True), sinks[hf]) for j, (n, hf) in enumerate(jobs)]
    pts = [jnp.exp2(sts[j] - ms[j]).astype(bf16) for j in range(len(jobs))]
    accs = [jnp.dot(vts[n], pts[j], preferred_element_type=f32) for j, (n, hf) in enumerate(jobs)]
    ots = [accs[j][hf * HEAD_DIM:(hf + 1) * HEAD_DIM] / (accs[j][nf:nf + 1] + jnp.exp2(sinks[hf] - ms[j]))
           for j, (n, hf) in enumerate(jobs)]
    for n in range(QBLOCKS_W):
        ot = jnp.concatenate([ots[2 * n], ots[2 * n + 1]], axis=0)
        o_ref[0, n * TQ_W:(n + 1) * TQ_W, :] = ot.T.astype(o_ref.dtype)


def _win_attn(sink2, qt, k, vt, bias, *, win, lead, kv_group):
    B, width, T = qt.shape
    n_pairs = width // LANES
    tq_step = QBLOCKS_W * TQ_W
    return pl.pallas_call(
        functools.partial(_win_attn_kernel, seq=T, win=win, lead=lead),
        grid=(B, n_pairs, T // tq_step),
        in_specs=[
            pl.BlockSpec(memory_space=pltpu.SMEM),
            pl.BlockSpec((1, LANES, tq_step), lambda b, p, s: (b, p, s)),
            pl.BlockSpec((1, T, LANES), lambda b, p, s: (b, 0, p // kv_group)),
            pl.BlockSpec((1, 1, T // LANES, VT_ROWS, LANES), lambda b, p, s: (b, p // kv_group, 0, 0, 0)),
            pl.BlockSpec((3, 2, win, TQ_W), lambda b, p, s: (0, p, 0, 0)),
        ],
        out_specs=pl.BlockSpec((1, tq_step, LANES), lambda b, p, s: (b, s, p)),
        out_shape=jax.ShapeDtypeStruct((B, T, width), jnp.bfloat16),
        compiler_params=_params(dimension_semantics=("parallel", "parallel", "arbitrary")),
        name="win_attn",
    )(sink2, qt, k, vt, bias)


def _merge_kernel(x_ref, oa_ref, ob_ref, oc_ref, za_ref, zb_ref, zc_ref, g_ref,
                  wa_ref, wb_ref, wc_ref, wo_ref, y_ref):
    bf16, f32 = jnp.bfloat16, jnp.float32

    def sigmoid(v):
        return 0.5 * jnp.tanh(0.5 * v) + 0.5

    def branch(o_ref, z_ref, w_ref, i):
        z = z_ref[0].astype(f32)
        u = o_ref[0].astype(f32) * (z * sigmoid(z))
        y = jnp.dot(u.astype(bf16), w_ref[...], preferred_element_type=f32)
        gate = g_ref[0, :, i * D_MODEL:(i + 1) * D_MODEL].astype(f32)
        return sigmoid(gate) * y

    m = branch(oa_ref, za_ref, wa_ref, 0) + branch(ob_ref, zb_ref, wb_ref, 1) + branch(oc_ref, zc_ref, wc_ref, 2)
    y_ref[0] = x_ref[0] + jnp.dot(m.astype(bf16), wo_ref[...], preferred_element_type=f32)


def _merge(x, oa, ob, oc, za, zb, zc, g, wa, wb, wc, wo):
    B, T, _ = x.shape
    tok = lambda w: pl.BlockSpec((1, TM, w), lambda b, t: (b, t, 0))
    const2 = lambda a: _resident(a.shape, lambda b, t: (0, 0))
    return pl.pallas_call(
        _merge_kernel,
        grid=(B, T // TM),
        in_specs=[tok(D_MODEL), tok(W_A), tok(W_B), tok(W_C), tok(W_A), tok(W_B), tok(W_C), tok(3 * D_MODEL),
                  const2(wa), const2(wb), const2(wc), const2(wo)],
        out_specs=tok(D_MODEL),
        out_shape=jax.ShapeDtypeStruct(x.shape, x.dtype),
        compiler_params=_params(dimension_semantics=("parallel", "parallel")),
        name="merge",
    )(x, oa, ob, oc, za, zb, zc, g, wa, wb, wc, wo)


def _split3_bf16(v):
    hi = v.astype(jnp.bfloat16)
    r1 = v - hi.astype(jnp.float32)
    mid = r1.astype(jnp.bfloat16)
    lo = (r1 - mid.astype(jnp.float32)).astype(jnp.bfloat16)
    return hi, mid, lo


def _diff_constants():
    c = np.asarray(_alibi_slopes(H_A), np.float32) * np.float32(LOG2E)
    col = jnp.asarray(c)[:, None] * jnp.arange(TK_A, dtype=jnp.float32)[None, :]
    pieces = jnp.stack(_split3_bf16(col), axis=-1)
    kbias = jnp.zeros((H_A, 2, TK_A, LANES), jnp.bfloat16)
    kbias = kbias.at[:, 0, :, HEAD_DIM:HEAD_DIM + N_BIAS_LANES].set(pieces)
    kbias = kbias.at[:, 1, :, 0:N_BIAS_LANES].set(pieces)
    dist = np.abs(np.arange(TQ_A)[None, :] - np.arange(TK_A)[:, None]).astype(np.float32)
    dbias = jnp.asarray(-c[:, None, None] * dist[None])
    return jnp.asarray(c), kbias, dbias


def _window_bias_c():
    slopes = np.asarray(_alibi_slopes(H_C), np.float64)
    di = np.arange(TQ_W)[None, :]
    dj = np.arange(WIN_C)[:, None]
    tiles = []
    for lead in (0, WINDOW, WIN_C - TQ_W):
        dist = np.abs(lead + di - dj)
        tile = np.where(dist <= WINDOW, -(slopes[:, None, None] * LOG2E) * dist[None], MASKED)
        tiles.append(tile)
    return jnp.asarray(np.stack(tiles).astype(np.float32))


def _window_bias_b(rpb):
    q_rows = TQ_W // GRID_W
    w_rows = WIN_B // GRID_W
    qc = np.arange(GRID_W)[:, None]
    kc = np.arange(GRID_W)[None, :]
    cstart = np.clip(qc - KW // 2, 0, GRID_W - KW)
    col_ok = (kc >= cstart) & (kc < cstart + KW)
    dc = kc - qc + (KW - 1)
    col_sel = (dc[:, :, None] == np.arange(2 * KW - 1)) & col_ok[:, :, None]
    qr = np.arange(q_rows)[:, None]
    wr = np.arange(w_rows)[None, :]
    row_sel, row_oks = [], []
    for lead_rows, rs in ((0, 0 * qr), (MAX_KH // 2, qr), (w_rows - q_rows, 0 * qr + MAX_KH // 2)):
        row_ok = (wr >= rs) & (wr < rs + MAX_KH)
        dr = wr - (lead_rows + qr) + (MAX_KH - 1)
        row_sel.append((dr[:, :, None] == np.arange(2 * MAX_KH - 1)) & row_ok[:, :, None])
        row_oks.append(row_ok)
    row_sel = jnp.asarray(np.stack(row_sel), jnp.float32)
    vals = jnp.einsum("vrwd,hde,cke->vhwkrc", row_sel, rpb.astype(jnp.float32) * LOG2E,
                      jnp.asarray(col_sel, jnp.float32), precision=lax.Precision.HIGHEST)
    row_ok_wr = np.stack(row_oks).transpose(0, 2, 1)
    ok = row_ok_wr[:, None, :, None, :, None] & col_ok.T[None, None, None, :, None, :]
    return jnp.where(jnp.asarray(ok), vals, MASKED).reshape(3, H_B, WIN_B, TQ_W)


def _layer_weights(l, w_in, norm_g, qk_gain_a, qk_gain_b, qk_gain_c):
    bf16, f32 = jnp.bfloat16, jnp.float32
    w = w_in[l]
    sizes = (W_A,) * 4 + (W_B,) * 4 + (W_C, KV_C * HEAD_DIM, KV_C * HEAD_DIM, W_C) + (D_MODEL,) * 3
    offs = np.concatenate([[0], np.cumsum(sizes)])
    names = ("qa", "ka", "va", "za", "qb", "kb", "vb", "zb", "qc", "kc", "vc", "zc", "ga", "gb", "gc")
    cols = {n: w[:, offs[i]:offs[i + 1]] for i, n in enumerate(names)}
    dup = lambda a: jnp.concatenate([a[:, :HEAD_DIM], a[:, :HEAD_DIM], a[:, HEAD_DIM:], a[:, HEAD_DIM:]], axis=1)
    cols["kc"] = dup(cols["kc"])
    cols["vc"] = dup(cols["vc"])
    wtok = jnp.concatenate([cols[n] for n, _ in _TOK_SIZES], axis=1).astype(bf16)
    wfeat = jnp.concatenate([cols[n] for n, _ in _FEAT_SIZES], axis=1).T.astype(bf16)
    qscale = QK_SCALE * LOG2E
    tile8 = lambda g: jnp.tile(g.astype(f32), W_A // HEAD_DIM)
    zero = jnp.zeros((W_A,), f32)
    gains = jnp.stack([tile8(qk_gain_a[l, 1]), tile8(qk_gain_b[l, 1]), tile8(qk_gain_c[l, 1])] + [zero] * 5)
    gfeat = jnp.stack([qk_gain_a[l, 0], qk_gain_b[l, 0], qk_gain_c[l, 0]]).astype(f32)[:, :, None] * qscale
    ng = norm_g[l].astype(f32)[None, :]
    return ng, wtok, wfeat, gains, gfeat


def _trunk(x, layers, consts):
    cvec, kbias, dbias, gmat, bias_c, no_sink = consts
    for lw in layers:
        (ng, wtok, wfeat, gains, gfeat, lam, g2, bias_b, sink2, wa, wb, wc, wo) = lw
        (qat, ka, vat, za, qbt, kb, vbt, zb, qct, kc, vct, zc, g) = _inproj(x, ng, wtok, wfeat, gmat, gains, gfeat)
        oa = _diff_attn(cvec, lam, qat, ka, vat, kbias, dbias, g2)
        ob = _win_attn(no_sink, qbt, kb, vbt, bias_b, win=WIN_B, lead=(MAX_KH // 2) * GRID_W, kv_group=1)
        oc = _win_attn(sink2, qct, kc, vct, bias_c, win=WIN_C, lead=WINDOW, kv_group=2)
        x = _merge(x, oa, ob, oc, za, zb, zc, g, wa, wb, wc, wo)
    return x


def kernel(x_prompt, x_sample, norm_g, w_in, qk_gain_a, lambda_a, subln_g_a, qk_gain_b, rpb_b,
           qk_gain_c, sink_c, w_proj_a, w_proj_b, w_proj_c, w_out):
    bf16, f32 = jnp.bfloat16, jnp.float32
    for x in (x_prompt, x_sample):
        T = x.shape[1]
        assert T % (STEPS_A * TK_A) == 0 and T >= 3 * TQ_W and T >= WIN_B and x.shape[2] == D_MODEL
    cvec, kbias, dbias = _diff_constants()
    gidx = np.arange(NORM_TILE) // HEAD_DIM
    gmat = jnp.asarray((gidx[:, None] == gidx[None, :]).astype(np.float32) / HEAD_DIM, bf16)
    consts = (cvec, kbias, dbias, gmat, _window_bias_c(), jnp.full((H_B,), MASKED, f32))
    layers = []
    for l in range(DEPTH):
        lam_init = 0.8 - 0.6 * math.exp(-0.3 * l)
        lv = lambda_a[l].astype(f32)
        lam = (jnp.exp(jnp.sum(lv[0] * lv[1])) - jnp.exp(jnp.sum(lv[2] * lv[3])) + lam_init).reshape(1)
        g2 = (subln_g_a[l].astype(f32) * (1.0 - lam_init))[:, None]
        layers.append(_layer_weights(l, w_in, norm_g, qk_gain_a, qk_gain_b, qk_gain_c)
                      + (lam, g2, _window_bias_b(rpb_b[l]), sink_c[l].astype(f32) * LOG2E,
                         w_proj_a[l].astype(bf16), w_proj_b[l].astype(bf16), w_proj_c[l].astype(bf16),
                         w_out[l].astype(bf16)))
    return (_trunk(x_prompt, layers, consts), _trunk(x_sample, layers, consts))
```

```python
import functools
import math

import numpy as np
import jax
import jax.numpy as jnp
from jax import lax
from jax.experimental import pallas as pl
from jax.experimental.pallas import tpu as pltpu

D_MODEL = 1024
DEPTH = 2
HEAD_DIM = 64
EPS = 1e-6
H_A = 4
W_A = H_A * 2 * HEAD_DIM
H_B = 8
W_B = H_B * HEAD_DIM
GRID_W = 64
MAX_KH = 8
KW = 16
H_C = 8
KV_C = 2
W_C = H_C * HEAD_DIM
WINDOW = 128

LOG2E = math.log2(math.e)
QK_SCALE = HEAD_DIM ** -0.5
MASKED = -1e30

LANES = 128
TM = 512
NORM_TILE = 256
TQ_A = TM
TK_A = TM
STEPS_A = 16
MIN_STEPS_A = 8
TQ_W = 256
QBLOCKS_W = 8
WIN_B = (MAX_KH + 4) * GRID_W
WIN_C = TQ_W + 2 * WINDOW
N_BIAS_LANES = 3
BF16_ROWS = 16
VT_ROWS = 2 * HEAD_DIM + BF16_ROWS
VMEM_LIMIT_BYTES = 56 * 1024 * 1024


def _offsets(sizes):
    offs, o = {}, 0
    for name, size in sizes:
        offs[name] = (o, o + size)
        o += size
    return offs


_TOK_SIZES = (("ka", W_A), ("za", W_A), ("kb", W_B), ("zb", W_B), ("kc", 4 * HEAD_DIM), ("zc", W_C),
              ("ga", D_MODEL), ("gb", D_MODEL), ("gc", D_MODEL))
_FEAT_SIZES = (("qa", W_A), ("va", W_A), ("qb", W_B), ("vb", W_B), ("qc", W_C), ("vc", 4 * HEAD_DIM))
_TOK_OFF = _offsets(_TOK_SIZES)
_FEAT_OFF = _offsets(_FEAT_SIZES)


def _alibi_slopes(n):
    return [2.0 ** (-8.0 * (i + 1) / n) for i in range(n)]


def _params(**kw):
    return pltpu.CompilerParams(vmem_limit_bytes=VMEM_LIMIT_BYTES, **kw)


def _resident(block_shape, index_map):
    return pl.BlockSpec(block_shape, index_map, pipeline_mode=pl.Buffered(1))


def _inproj_kernel(x_ref, ng_ref, wtok_ref, wfeat_ref, gmat_ref, gains_ref, gfeat_ref,
                   qat_ref, ka_ref, vat_ref, za_ref, qbt_ref, kb_ref, vbt_ref, zb_ref,
                   qct_ref, kc_ref, vct_ref, zc_ref, g_ref):
    bf16, f32 = jnp.bfloat16, jnp.float32
    x = x_ref[0]
    ms = jnp.mean(x * x, axis=-1, keepdims=True)
    h = (x * lax.rsqrt(ms + EPS) * ng_ref[...]).astype(bf16)

    def proj(name):
        c0, c1 = _TOK_OFF[name]
        return jnp.dot(h, wtok_ref[:, c0:c1], preferred_element_type=f32)

    def normed(name, gain_row):
        p = proj(name)
        w = p.shape[-1]
        pp = (p * p).astype(bf16)
        ss = jnp.concatenate([jnp.dot(pp[:, c0:c0 + NORM_TILE], gmat_ref[...], preferred_element_type=f32)
                              for c0 in range(0, w, NORM_TILE)], axis=1)
        return (p * lax.rsqrt(ss + EPS) * gains_ref[gain_row:gain_row + 1, :w]).astype(bf16)

    ka_ref[0] = normed("ka", 0)
    za_ref[0] = proj("za").astype(bf16)
    kb_ref[0] = normed("kb", 1)
    zb_ref[0] = proj("zb").astype(bf16)
    kc_ref[0] = normed("kc", 2)
    zc_ref[0] = proj("zc").astype(bf16)
    for i, name in enumerate(("ga", "gb", "gc")):
        g_ref[0, :, i * D_MODEL:(i + 1) * D_MODEL] = proj(name).astype(bf16)

    def proj_t(name):
        r0, r1 = _FEAT_OFF[name]
        return lax.dot_general(wfeat_ref[r0:r1, :], h, (((1,), (1,)), ((), ())), preferred_element_type=f32)

    def put_q(name, gain_idx, out_ref):
        pf = proj_t(name)
        for g in range(pf.shape[0] // HEAD_DIM):
            blk = pf[g * HEAD_DIM:(g + 1) * HEAD_DIM]
            r = lax.rsqrt(jnp.mean(blk * blk, axis=0, keepdims=True) + EPS)
            out_ref[0, g * HEAD_DIM:(g + 1) * HEAD_DIM, :] = (blk * r * gfeat_ref[gain_idx]).astype(bf16)

    put_q("qa", 0, qat_ref)
    put_q("qb", 1, qbt_ref)
    put_q("qc", 2, qct_ref)

    pad_row = lax.broadcasted_iota(jnp.int32, (BF16_ROWS, TM), 0)
    ones_row = jnp.where(pad_row == 0, 1.0, 0.0).astype(bf16)
    nf = 2 * HEAD_DIM
    pf = proj_t("va")
    for hh in range(H_A):
        vat_ref[0, hh, 0, :nf, :] = pf[hh * nf:(hh + 1) * nf].astype(bf16)
        vat_ref[0, hh, 0, nf:, :] = ones_row
    for name, out_ref in (("vb", vbt_ref), ("vc", vct_ref)):
        pf = proj_t(name)
        for p in range(pf.shape[0] // nf):
            for s in range(TM // LANES):
                out_ref[0, p, s, :nf, :] = pf[p * nf:(p + 1) * nf, s * LANES:(s + 1) * LANES].astype(bf16)
                out_ref[0, p, s, nf:, :] = ones_row[:, :LANES]


def _inproj(x, ng, wtok, wfeat, gmat, gains, gfeat):
    B, T, _ = x.shape
    nt = T // TM
    bf16 = jnp.bfloat16
    tok = lambda w: pl.BlockSpec((1, TM, w), lambda b, t: (b, t, 0))
    feat = lambda w: pl.BlockSpec((1, w, TM), lambda b, t: (b, 0, t))
    slabs = TM // LANES
    vt_slabs = lambda n: pl.BlockSpec((1, n, slabs, VT_ROWS, LANES), lambda b, t: (b, 0, t, 0, 0))
    const = lambda a: _resident(a.shape, lambda b, t: (0,) * a.ndim)
    out_shape = (
        jax.ShapeDtypeStruct((B, W_A, T), bf16),
        jax.ShapeDtypeStruct((B, T, W_A), bf16),
        jax.ShapeDtypeStruct((B, H_A, nt, VT_ROWS, TM), bf16),
        jax.ShapeDtypeStruct((B, T, W_A), bf16),
        jax.ShapeDtypeStruct((B, W_B, T), bf16),
        jax.ShapeDtypeStruct((B, T, W_B), bf16),
        jax.ShapeDtypeStruct((B, H_B // 2, T // LANES, VT_ROWS, LANES), bf16),
        jax.ShapeDtypeStruct((B, T, W_B), bf16),
        jax.ShapeDtypeStruct((B, W_C, T), bf16),
        jax.ShapeDtypeStruct((B, T, 4 * HEAD_DIM), bf16),
        jax.ShapeDtypeStruct((B, KV_C, T // LANES, VT_ROWS, LANES), bf16),
        jax.ShapeDtypeStruct((B, T, W_C), bf16),
        jax.ShapeDtypeStruct((B, T, 3 * D_MODEL), bf16),
    )
    out_specs = (
        feat(W_A),
        tok(W_A),
        pl.BlockSpec((1, H_A, 1, VT_ROWS, TM), lambda b, t: (b, 0, t, 0, 0)),
        tok(W_A), feat(W_B), tok(W_B), vt_slabs(H_B // 2), tok(W_B), feat(W_C),
        tok(4 * HEAD_DIM), vt_slabs(KV_C), tok(W_C), tok(3 * D_MODEL),
    )
    return pl.pallas_call(
        _inproj_kernel,
        grid=(B, nt),
        in_specs=[pl.BlockSpec((1, TM, D_MODEL), lambda b, t: (b, t, 0)),
                  const(ng), const(wtok), const(wfeat), const(gmat), const(gains), const(gfeat)],
        out_specs=out_specs,
        out_shape=out_shape,
        compiler_params=_params(dimension_semantics=("parallel", "parallel")),
        name="inproj",
    )(x, ng, wtok, wfeat, gmat, gains, gfeat)


def _steps_per_iteration(nk):
    steps = STEPS_A
    while steps > MIN_STEPS_A and nk < 2 * steps:
        steps //= 2
    return steps


def _diff_attn_kernel(c_ref, lam_ref, trips_ref, qt_ref, k_ref, vt_ref, kbias_ref, dbias_ref, g2_ref, o_ref,
                      kp_ref, qv_ref, sa_ref, sb_ref, ta_ref, tb_ref, m_ref, acc_ref, *, seq):
    bf16, f32 = jnp.bfloat16, jnp.float32
    h = pl.program_id(1)
    qi = pl.program_id(2)
    nk = seq // TK_A
    n_steps = _steps_per_iteration(nk)
    c = c_ref[h]

    @pl.when(qi == 0)
    def _():
        lane = lax.broadcasted_iota(jnp.int32, (TK_A, LANES), 1)
        kb0 = kbias_ref[0, 0].astype(f32)
        kb1 = kbias_ref[0, 1].astype(f32)

        def body(j, carry):
            rows = pl.ds(pl.multiple_of(j * TK_A, TK_A), TK_A)
            kblk = k_ref[0, rows, :].astype(f32)
            kp_ref[0, rows, :] = jnp.where(lane < HEAD_DIM, kblk, kb0).astype(bf16)
            kp_ref[1, rows, :] = jnp.where(lane >= HEAD_DIM, kblk, kb1).astype(bf16)
            return carry

        lax.fori_loop(0, nk, body, 0)

    row = lax.broadcasted_iota(jnp.int32, (2 * HEAD_DIM, TQ_A), 0)
    qblk = qt_ref[0].astype(f32)
    for sub in range(2):
        own = (row < HEAD_DIM) if sub == 0 else (row >= HEAD_DIM)
        b0 = HEAD_DIM if sub == 0 else 0
        brow = (row >= b0) & (row < b0 + N_BIAS_LANES)
        base = jnp.where(own, qblk, 0.0)
        for var, sigma in enumerate((1.0, -1.0, 0.0)):
            qv_ref[3 * sub + var] = jnp.where(brow, sigma, base).astype(bf16)

    m_ref[...] = jnp.full(m_ref.shape, MASKED, f32)
    acc_ref[...] = jnp.zeros(acc_ref.shape, f32)

    i0 = qi * TQ_A
    jd = i0 // TK_A
    di = lax.broadcasted_iota(jnp.int32, (1, TQ_A), 1)

    def update(sub, t, tmax, rvec, vt):
        m_old = m_ref[sub]
        m_new = jnp.maximum(m_old, tmax + rvec)
        alpha = jnp.exp2(m_old - m_new)
        p = jnp.exp2(t - (m_new - rvec))
        acc_ref[sub] = alpha * acc_ref[sub] + jnp.dot(vt, p.astype(bf16), preferred_element_type=f32)
        m_ref[sub] = m_new

    def put_scores(s_ref, t_ref, sub, t):
        s_ref[sub] = t
        t_ref[sub] = jnp.max(t, axis=0, keepdims=True)

    def block_of(step):
        jj = step - 1
        j = jnp.where(step == 0, jd, jj + (jj >= jd).astype(jnp.int32))
        return j, j > jd

    def produce(bufs, step):
        s_ref, t_ref = bufs
        j, after = block_of(step)
        var = after.astype(jnp.int32)
        rows = pl.ds(pl.multiple_of(j * TK_A, TK_A), TK_A)
        for sub in range(2):
            put_scores(s_ref, t_ref, sub,
                       jnp.dot(kp_ref[sub, rows, :], qv_ref[3 * sub + var], preferred_element_type=f32))

    def consume(bufs, step):
        s_ref, t_ref = bufs
        j, after = block_of(step)
        sgn_c = jnp.where(j == jd, 0.0, jnp.where(after, c, -c)).astype(f32)
        rvec = sgn_c * (i0 - j * TK_A + di).astype(f32)
        vt = vt_ref[0, 0, j]
        for sub in range(2):
            update(sub, s_ref[sub], t_ref[sub], rvec, vt)

    buf_a = (sa_ref, ta_ref)
    buf_b = (sb_ref, tb_ref)

    rows_d = pl.ds(pl.multiple_of(jd * TK_A, TK_A), TK_A)
    for sub in range(2):
        put_scores(sa_ref, ta_ref, sub,
                   jnp.dot(kp_ref[sub, rows_d, :], qv_ref[3 * sub + 2], preferred_element_type=f32) + dbias_ref[0])

    def steps(s0, produce_last):
        for k in range(0, n_steps, 2):
            produce(buf_b, s0 + k + 1)
            consume(buf_a, s0 + k)
            if produce_last or k + 2 < n_steps:
                produce(buf_a, s0 + k + 2)
            consume(buf_b, s0 + k + 1)

    def body(i, carry):
        steps(n_steps * i, True)
        return carry

    lax.fori_loop(0, trips_ref[0], body, 0)
    steps(nk - n_steps, False)

    nf = 2 * HEAD_DIM
    o = (acc_ref[0, :nf, :] / acc_ref[0, nf:nf + 1, :]
         - lam_ref[0] * (acc_ref[1, :nf, :] / acc_ref[1, nf:nf + 1, :]))
    y = o * lax.rsqrt(jnp.mean(o * o, axis=0, keepdims=True) + EPS) * g2_ref[...]
    o_ref[0] = y.T.astype(o_ref.dtype)


def _diff_attn(cvec, lam, qat, ka, vat, kbias, dbias, g2):
    B, _, T = qat.shape
    nq = T // TQ_A
    nkb = T // TK_A
    smem = pl.BlockSpec(memory_space=pltpu.SMEM)
    trips = jnp.full((1,), nkb // _steps_per_iteration(nkb) - 1, jnp.int32)
    return pl.pallas_call(
        functools.partial(_diff_attn_kernel, seq=T),
        grid=(B, H_A, nq),
        in_specs=[
            smem, smem, smem,
            pl.BlockSpec((1, 2 * HEAD_DIM, TQ_A), lambda b, h, q: (b, h, q)),
            pl.BlockSpec((1, T, 2 * HEAD_DIM), lambda b, h, q: (b, 0, h)),
            pl.BlockSpec((1, 1, nkb, VT_ROWS, TK_A), lambda b, h, q: (b, h, 0, 0, 0)),
            pl.BlockSpec((1, 2, TK_A, LANES), lambda b, h, q: (h, 0, 0, 0)),
            pl.BlockSpec((1, TK_A, TQ_A), lambda b, h, q: (h, 0, 0)),
            pl.BlockSpec((2 * HEAD_DIM, 1), lambda b, h, q: (0, 0)),
        ],
        out_specs=pl.BlockSpec((1, TQ_A, 2 * HEAD_DIM), lambda b, h, q: (b, q, h)),
        out_shape=jax.ShapeDtypeStruct((B, T, W_A), jnp.bfloat16),
        scratch_shapes=[
            pltpu.VMEM((2, T, LANES), jnp.bfloat16),
            pltpu.VMEM((6, 2 * HEAD_DIM, TQ_A), jnp.bfloat16),
            pltpu.VMEM((2, TK_A, TQ_A), jnp.float32),
            pltpu.VMEM((2, TK_A, TQ_A), jnp.float32),
            pltpu.VMEM((2, 1, TQ_A), jnp.float32),
            pltpu.VMEM((2, 1, TQ_A), jnp.float32),
            pltpu.VMEM((2, 1, TQ_A), jnp.float32),
            pltpu.VMEM((2, VT_ROWS, TQ_A), jnp.float32),
        ],
        compiler_params=_params(dimension_semantics=("parallel", "parallel", "arbitrary")),
        name="diff_attn",
    )(cvec, lam, trips, qat, ka, vat, kbias, dbias, g2)


def _win_attn_kernel(sink_ref, qt_ref, k_ref, vt_ref, bias_ref, o_ref, *, seq, win, lead):
    bf16, f32 = jnp.bfloat16, jnp.float32
    pair = pl.program_id(1)
    step = pl.program_id(2)
    nq = seq // TQ_W
    nf = 2 * HEAD_DIM
    row = lax.broadcasted_iota(jnp.int32, (nf, TQ_W), 0)
    sinks = [sink_ref[2 * pair + hf] for hf in range(2)]

    jobs = [(n, hf) for n in range(QBLOCKS_W) for hf in range(2)]
    kws, vts, variants, qts = [], [], [], []
    for n in range(QBLOCKS_W):
        qi = step * QBLOCKS_W + n
        ws = pl.multiple_of(jnp.clip(qi * TQ_W - lead, 0, seq - win), LANES)
        variants.append(jnp.where(qi == 0, 0, jnp.where(qi == nq - 1, 2, 1)))
        kws.append(k_ref[0, pl.ds(ws, win), :])
        slab = ws // LANES
        vts.append(jnp.concatenate([vt_ref[0, 0, slab + c] for c in range(win // LANES)], axis=1))
        qts.append(qt_ref[0, :, n * TQ_W:(n + 1) * TQ_W].astype(f32))
    qms = [jnp.where((row < HEAD_DIM) if hf == 0 else (row >= HEAD_DIM), qts[n], 0.0).astype(bf16)
           for n, hf in jobs]
    sts = [jnp.dot(kws[n], qms[j], preferred_element_type=f32) + bias_ref[variants[n], hf]
           for j, (n, hf) in enumerate(jobs)]
    ms = [jnp.maximum(jnp.max(sts[j], axis=0, keepdims=True), sinks[hf]) for j, (n, hf) in enumerate(jobs)]
    pts = [jnp.exp2(sts[j] - ms[j]).astype(bf16) for j in range(len(jobs))]
    accs = [jnp.dot(vts[n], pts[j], preferred_element_type=f32) for j, (n, hf) in enumerate(jobs)]
    ots = [accs[j][hf * HEAD_DIM:(hf + 1) * HEAD_DIM] / (accs[j][nf:nf + 1] + jnp.exp2(sinks[hf] - ms[j]))
           for j, (n, hf) in enumerate(jobs)]
    for n in range(QBLOCKS_W):
        ot = jnp.concatenate([ots[2 * n], ots[2 * n + 1]], axis=0)
        o_ref[0, n * TQ_W:(n + 1) * TQ_W, :] = ot.T.astype(o_ref.dtype)


def _win_attn(sink2, qt, k, vt, bias, *, win, lead, kv_group):
    B, width, T = qt.shape
    n_pairs = width // LANES
    tq_step = QBLOCKS_W * TQ_W
    return pl.pallas_call(
        functools.partial(_win_attn_kernel, seq=T, win=win, lead=lead),
        grid=(B, n_pairs, T // tq_step),
        in_specs=[
            pl.BlockSpec(memory_space=pltpu.SMEM),
            pl.BlockSpec((1, LANES, tq_step), lambda b, p, s: (b, p, s)),
            pl.BlockSpec((1, T, LANES), lambda b, p, s: (b, 0, p // kv_group)),
            pl.BlockSpec((1, 1, T // LANES, VT_ROWS, LANES), lambda b, p, s: (b, p // kv_group, 0, 0, 0)),
            pl.BlockSpec((3, 2, win, TQ_W), lambda b, p, s: (0, p, 0, 0)),
        ],
        out_specs=pl.BlockSpec((1, tq_step, LANES), lambda b, p, s: (b, s, p)),
        out_shape=jax.ShapeDtypeStruct((B, T, width), jnp.bfloat16),
        compiler_params=_params(dimension_semantics=("parallel", "parallel", "arbitrary")),
        name="win_attn",
    )(sink2, qt, k, vt, bias)


def _merge_kernel(x_ref, oa_ref, ob_ref, oc_ref, za_ref, zb_ref, zc_ref, g_ref,
                  wa_ref, wb_ref, wc_ref, wo_ref, y_ref):
    bf16, f32 = jnp.bfloat16, jnp.float32

    def sigmoid(v):
        return 0.5 * jnp.tanh(0.5 * v) + 0.5

    def branch(o_ref, z_ref, w_ref, i):
        z = z_ref[0].astype(f32)
        u = o_ref[0].astype(f32) * (z * sigmoid(z))
        y = jnp.dot(u.astype(bf16), w_ref[...], preferred_element_type=f32)
        gate = g_ref[0, :, i * D_MODEL:(i + 1) * D_MODEL].astype(f32)
        return sigmoid(gate) * y

    m = branch(oa_ref, za_ref, wa_ref, 0) + branch(ob_ref, zb_ref, wb_ref, 1) + branch(oc_ref, zc_ref, wc_ref, 2)
    y_ref[0] = x_ref[0] + jnp.dot(m.astype(bf16), wo_ref[...], preferred_element_type=f32)


def _merge(x, oa, ob, oc, za, zb, zc, g, wa, wb, wc, wo):
    B, T, _ = x.shape
    tok = lambda w: pl.BlockSpec((1, TM, w), lambda b, t: (b, t, 0))
    const2 = lambda a: _resident(a.shape, lambda b, t: (0, 0))
    return pl.pallas_call(
        _merge_kernel,
        grid=(B, T // TM),
        in_specs=[tok(D_MODEL), tok(W_A), tok(W_B), tok(W_C), tok(W_A), tok(W_B), tok(W_C), tok(3 * D_MODEL),
                  const2(wa), const2(wb), const2(wc), const2(wo)],
        out_specs=tok(D_MODEL),
        out_shape=jax.ShapeDtypeStruct(x.shape, x.dtype),
        compiler_params=_params(dimension_semantics=("parallel", "parallel")),
        name="merge",
    )(x, oa, ob, oc, za, zb, zc, g, wa, wb, wc, wo)


def _split3_bf16(v):
    hi = v.astype(jnp.bfloat16)
    r1 = v - hi.astype(jnp.float32)
    mid = r1.astype(jnp.bfloat16)
    lo = (r1 - mid.astype(jnp.float32)).astype(jnp.bfloat16)
    return hi, mid, lo


def _diff_constants():
    c = np.asarray(_alibi_slopes(H_A), np.float32) * np.float32(LOG2E)
    col = jnp.asarray(c)[:, None] * jnp.arange(TK_A, dtype=jnp.float32)[None, :]
    pieces = jnp.stack(_split3_bf16(col), axis=-1)
    kbias = jnp.zeros((H_A, 2, TK_A, LANES), jnp.bfloat16)
    kbias = kbias.at[:, 0, :, HEAD_DIM:HEAD_DIM + N_BIAS_LANES].set(pieces)
    kbias = kbias.at[:, 1, :, 0:N_BIAS_LANES].set(pieces)
    dist = np.abs(np.arange(TQ_A)[None, :] - np.arange(TK_A)[:, None]).astype(np.float32)
    dbias = jnp.asarray(-c[:, None, None] * dist[None])
    return jnp.asarray(c), kbias, dbias


def _window_bias_c():
    slopes = np.asarray(_alibi_slopes(H_C), np.float64)
    di = np.arange(TQ_W)[None, :]
    dj = np.arange(WIN_C)[:, None]
    tiles = []
    for lead in (0, WINDOW, WIN_C - TQ_W):
        dist = np.abs(lead + di - dj)
        tile = np.where(dist <= WINDOW, -(slopes[:, None, None] * LOG2E) * dist[None], MASKED)
        tiles.append(tile)
    return jnp.asarray(np.stack(tiles).astype(np.float32))


def _window_bias_b(rpb):
    q_rows = TQ_W // GRID_W
    w_rows = WIN_B // GRID_W
    qc = np.arange(GRID_W)[:, None]
    kc = np.arange(GRID_W)[None, :]
    cstart = np.clip(qc - KW // 2, 0, GRID_W - KW)
    col_ok = (kc >= cstart) & (kc < cstart + KW)
    dc = kc - qc + (KW - 1)
    col_sel = (dc[:, :, None] == np.arange(2 * KW - 1)) & col_ok[:, :, None]
    qr = np.arange(q_rows)[:, None]
    wr = np.arange(w_rows)[None, :]
    row_sel, row_oks = [], []
    for lead_rows, rs in ((0, 0 * qr), (MAX_KH // 2, qr), (w_rows - q_rows, 0 * qr + MAX_KH // 2)):
        row_ok = (wr >= rs) & (wr < rs + MAX_KH)
        dr = wr - (lead_rows + qr) + (MAX_KH - 1)
        row_sel.append((dr[:, :, None] == np.arange(2 * MAX_KH - 1)) & row_ok[:, :, None])
        row_oks.append(row_ok)
    row_sel = jnp.asarray(np.stack(row_sel), jnp.float32)
    vals = jnp.einsum("vrwd,hde,cke->vhwkrc", row_sel, rpb.astype(jnp.float32) * LOG2E,
                      jnp.asarray(col_sel, jnp.float32), precision=lax.Precision.HIGHEST)
    row_ok_wr = np.stack(row_oks).transpose(0, 2, 1)
    ok = row_ok_wr[:, None, :, None, :, None] & col_ok.T[None, None, None, :, None, :]
    return jnp.where(jnp.asarray(ok), vals, MASKED).reshape(3, H_B, WIN_B, TQ_W)


def _layer_weights(l, w_in, norm_g, qk_gain_a, qk_gain_b, qk_gain_c):
    bf16, f32 = jnp.bfloat16, jnp.float32
    w = w_in[l]
    sizes = (W_A,) * 4 + (W_B,) * 4 + (W_C, KV_C * HEAD_DIM, KV_C * HEAD_DIM, W_C) + (D_MODEL,) * 3
    offs = np.concatenate([[0], np.cumsum(sizes)])
    names = ("qa", "ka", "va", "za", "qb", "kb", "vb", "zb", "qc", "kc", "vc", "zc", "ga", "gb", "gc")
    cols = {n: w[:, offs[i]:offs[i + 1]] for i, n in enumerate(names)}
    dup = lambda a: jnp.concatenate([a[:, :HEAD_DIM], a[:, :HEAD_DIM], a[:, HEAD_DIM:], a[:, HEAD_DIM:]], axis=1)
    cols["kc"] = dup(cols["kc"])
    cols["vc"] = dup(cols["vc"])
    wtok = jnp.concatenate([cols[n] for n, _ in _TOK_SIZES], axis=1).astype(bf16)
    wfeat = jnp.concatenate([cols[n] for n, _ in _FEAT_SIZES], axis=1).T.astype(bf16)
    qscale = QK_SCALE * LOG2E
    tile8 = lambda g: jnp.tile(g.astype(f32), W_A // HEAD_DIM)
    zero = jnp.zeros((W_A,), f32)
    gains = jnp.stack([tile8(qk_gain_a[l, 1]), tile8(qk_gain_b[l, 1]), tile8(qk_gain_c[l, 1])] + [zero] * 5)
    gfeat = jnp.stack([qk_gain_a[l, 0], qk_gain_b[l, 0], qk_gain_c[l, 0]]).astype(f32)[:, :, None] * qscale
    ng = norm_g[l].astype(f32)[None, :]
    return ng, wtok, wfeat, gains, gfeat


def _trunk(x, layers, consts):
    cvec, kbias, dbias, gmat, bias_c, no_sink = consts
    for lw in layers:
        (ng, wtok, wfeat, gains, gfeat, lam, g2, bias_b, sink2, wa, wb, wc, wo) = lw
        (qat, ka, vat, za, qbt, kb, vbt, zb, qct, kc, vct, zc, g) = _inproj(x, ng, wtok, wfeat, gmat, gains, gfeat)
        oa = _diff_attn(cvec, lam, qat, ka, vat, kbias, dbias, g2)
        ob = _win_attn(no_sink, qbt, kb, vbt, bias_b, win=WIN_B, lead=(MAX_KH // 2) * GRID_W, kv_group=1)
        oc = _win_attn(sink2, qct, kc, vct, bias_c, win=WIN_C, lead=WINDOW, kv_group=2)
        x = _merge(x, oa, ob, oc, za, zb, zc, g, wa, wb, wc, wo)
    return x


def kernel(x_prompt, x_sample, norm_g, w_in, qk_gain_a, lambda_a, subln_g_a, qk_gain_b, rpb_b,
           qk_gain_c, sink_c, w_proj_a, w_proj_b, w_proj_c, w_out):
    bf16, f32 = jnp.bfloat16, jnp.float32
    for x in (x_prompt, x_sample):
        T = x.shape[1]
        nk = T // TK_A
        assert T % TK_A == 0 and nk % _steps_per_iteration(nk) == 0
        assert T % (QBLOCKS_W * TQ_W) == 0 and T >= 3 * TQ_W and T >= WIN_B and x.shape[2] == D_MODEL
    cvec, kbias, dbias = _diff_constants()
    gidx = np.arange(NORM_TILE) // HEAD_DIM
    gmat = jnp.asarray((gidx[:, None] == gidx[None, :]).astype(np.float32) / HEAD_DIM, bf16)
    consts = (cvec, kbias, dbias, gmat, _window_bias_c(), jnp.full((H_B,), MASKED, f32))
    layers = []
    for l in range(DEPTH):
        lam_init = 0.8 - 0.6 * math.exp(-0.3 * l)
        lv = lambda_a[l].astype(f32)
        lam = (jnp.exp(jnp.sum(lv[0] * lv[1])) - jnp.exp(jnp.sum(lv[2] * lv[3])) + lam_init).reshape(1)
        g2 = (subln_g_a[l].astype(f32) * (1.0 - lam_init))[:, None]
        layers.append(_layer_weights(l, w_in, norm_g, qk_gain_a, qk_gain_b, qk_gain_c)
                      + (lam, g2, _window_bias_b(rpb_b[l]), sink_c[l].astype(f32) * LOG2E,
                         w_proj_a[l].astype(bf16), w_proj_b[l].astype(bf16), w_proj_c[l].astype(bf16),
                         w_out[l].astype(bf16)))
    return (_trunk(x_prompt, layers, consts), _trunk(x_sample, layers, consts))
```

```python
import functools
import math

import numpy as np
import jax
import jax.numpy as jnp
from jax import lax
from jax.experimental import pallas as pl
from jax.experimental.pallas import tpu as pltpu

D_MODEL = 1024
DEPTH = 2
HEAD_DIM = 64
EPS = 1e-6
H_A = 4
W_A = H_A * 2 * HEAD_DIM
H_B = 8
W_B = H_B * HEAD_DIM
GRID_W = 64
MAX_KH = 8
KW = 16
H_C = 8
KV_C = 2
W_C = H_C * HEAD_DIM
WINDOW = 128

LOG2E = math.log2(math.e)
QK_SCALE = HEAD_DIM ** -0.5
MASKED = -1e30

LANES = 128
TM = 512
NORM_TILE = 256
TQ_A = TM
TK_A = TM
STEPS_A = 16
MIN_STEPS_A = 8
TQ_W = 256
QBLOCKS_W = 16
WIN_B = (MAX_KH + 4) * GRID_W
WIN_C = TQ_W + 2 * WINDOW
N_BIAS_LANES = 3
BF16_ROWS = 16
VT_ROWS = 2 * HEAD_DIM + BF16_ROWS
VMEM_LIMIT_BYTES = 56 * 1024 * 1024


def _offsets(sizes):
    offs, o = {}, 0
    for name, size in sizes:
        offs[name] = (o, o + size)
        o += size
    return offs


_TOK_SIZES = (("ka", W_A), ("za", W_A), ("kb", W_B), ("zb", W_B), ("kc", 4 * HEAD_DIM), ("zc", W_C),
              ("ga", D_MODEL), ("gb", D_MODEL), ("gc", D_MODEL))
_FEAT_SIZES = (("qa", W_A), ("va", W_A), ("qb", W_B), ("vb", W_B), ("qc", W_C), ("vc", 4 * HEAD_DIM))
_TOK_OFF = _offsets(_TOK_SIZES)
_FEAT_OFF = _offsets(_FEAT_SIZES)


def _alibi_slopes(n):
    return [2.0 ** (-8.0 * (i + 1) / n) for i in range(n)]


def _params(**kw):
    return pltpu.CompilerParams(vmem_limit_bytes=VMEM_LIMIT_BYTES, **kw)


def _resident(block_shape, index_map):
    return pl.BlockSpec(block_shape, index_map, pipeline_mode=pl.Buffered(1))


def _inproj_kernel(x_ref, ng_ref, wtok_ref, wfeat_ref, gmat_ref, gains_ref, gfeat_ref,
                   qat_ref, ka_ref, vat_ref, za_ref, qbt_ref, kb_ref, vbt_ref, zb_ref,
                   qct_ref, kc_ref, vct_ref, zc_ref, g_ref):
    bf16, f32 = jnp.bfloat16, jnp.float32
    x = x_ref[0]
    ms = jnp.mean(x * x, axis=-1, keepdims=True)
    h = (x * lax.rsqrt(ms + EPS) * ng_ref[...]).astype(bf16)

    def proj(name):
        c0, c1 = _TOK_OFF[name]
        return jnp.dot(h, wtok_ref[:, c0:c1], preferred_element_type=f32)

    def normed(name, gain_row):
        p = proj(name)
        w = p.shape[-1]
        pp = (p * p).astype(bf16)
        ss = jnp.concatenate([jnp.dot(pp[:, c0:c0 + NORM_TILE], gmat_ref[...], preferred_element_type=f32)
                              for c0 in range(0, w, NORM_TILE)], axis=1)
        return (p * lax.rsqrt(ss + EPS) * gains_ref[gain_row:gain_row + 1, :w]).astype(bf16)

    ka_ref[0] = normed("ka", 0)
    za_ref[0] = proj("za").astype(bf16)
    kb_ref[0] = normed("kb", 1)
    zb_ref[0] = proj("zb").astype(bf16)
    kc_ref[0] = normed("kc", 2)
    zc_ref[0] = proj("zc").astype(bf16)
    for i, name in enumerate(("ga", "gb", "gc")):
        g_ref[0, :, i * D_MODEL:(i + 1) * D_MODEL] = proj(name).astype(bf16)

    def proj_t(name):
        r0, r1 = _FEAT_OFF[name]
        return lax.dot_general(wfeat_ref[r0:r1, :], h, (((1,), (1,)), ((), ())), preferred_element_type=f32)

    def put_q(name, gain_idx, out_ref):
        pf = proj_t(name)
        for g in range(pf.shape[0] // HEAD_DIM):
            blk = pf[g * HEAD_DIM:(g + 1) * HEAD_DIM]
            r = lax.rsqrt(jnp.mean(blk * blk, axis=0, keepdims=True) + EPS)
            out_ref[0, g * HEAD_DIM:(g + 1) * HEAD_DIM, :] = (blk * r * gfeat_ref[gain_idx]).astype(bf16)

    put_q("qa", 0, qat_ref)
    put_q("qb", 1, qbt_ref)
    put_q("qc", 2, qct_ref)

    pad_row = lax.broadcasted_iota(jnp.int32, (BF16_ROWS, TM), 0)
    ones_row = jnp.where(pad_row == 0, 1.0, 0.0).astype(bf16)
    nf = 2 * HEAD_DIM
    pf = proj_t("va")
    for hh in range(H_A):
        vat_ref[0, hh, 0, :nf, :] = pf[hh * nf:(hh + 1) * nf].astype(bf16)
        vat_ref[0, hh, 0, nf:, :] = ones_row
    for name, out_ref in (("vb", vbt_ref), ("vc", vct_ref)):
        pf = proj_t(name)
        for p in range(pf.shape[0] // nf):
            for s in range(TM // LANES):
                out_ref[0, p, s, :nf, :] = pf[p * nf:(p + 1) * nf, s * LANES:(s + 1) * LANES].astype(bf16)
                out_ref[0, p, s, nf:, :] = ones_row[:, :LANES]


def _inproj(x, ng, wtok, wfeat, gmat, gains, gfeat):
    B, T, _ = x.shape
    nt = T // TM
    bf16 = jnp.bfloat16
    tok = lambda w: pl.BlockSpec((1, TM, w), lambda b, t: (b, t, 0))
    feat = lambda w: pl.BlockSpec((1, w, TM), lambda b, t: (b, 0, t))
    slabs = TM // LANES
    vt_slabs = lambda n: pl.BlockSpec((1, n, slabs, VT_ROWS, LANES), lambda b, t: (b, 0, t, 0, 0))
    const = lambda a: _resident(a.shape, lambda b, t: (0,) * a.ndim)
    out_shape = (
        jax.ShapeDtypeStruct((B, W_A, T), bf16),
        jax.ShapeDtypeStruct((B, T, W_A), bf16),
        jax.ShapeDtypeStruct((B, H_A, nt, VT_ROWS, TM), bf16),
        jax.ShapeDtypeStruct((B, T, W_A), bf16),
        jax.ShapeDtypeStruct((B, W_B, T), bf16),
        jax.ShapeDtypeStruct((B, T, W_B), bf16),
        jax.ShapeDtypeStruct((B, H_B // 2, T // LANES, VT_ROWS, LANES), bf16),
        jax.ShapeDtypeStruct((B, T, W_B), bf16),
        jax.ShapeDtypeStruct((B, W_C, T), bf16),
        jax.ShapeDtypeStruct((B, T, 4 * HEAD_DIM), bf16),
        jax.ShapeDtypeStruct((B, KV_C, T // LANES, VT_ROWS, LANES), bf16),
        jax.ShapeDtypeStruct((B, T, W_C), bf16),
        jax.ShapeDtypeStruct((B, T, 3 * D_MODEL), bf16),
    )
    out_specs = (
        feat(W_A),
        tok(W_A),
        pl.BlockSpec((1, H_A, 1, VT_ROWS, TM), lambda b, t: (b, 0, t, 0, 0)),
        tok(W_A), feat(W_B), tok(W_B), vt_slabs(H_B // 2), tok(W_B), feat(W_C),
        tok(4 * HEAD_DIM), vt_slabs(KV_C), tok(W_C), tok(3 * D_MODEL),
    )
    return pl.pallas_call(
        _inproj_kernel,
        grid=(B, nt),
        in_specs=[pl.BlockSpec((1, TM, D_MODEL), lambda b, t: (b, t, 0)),
                  const(ng), const(wtok), const(wfeat), const(gmat), const(gains), const(gfeat)],
        out_specs=out_specs,
        out_shape=out_shape,
        compiler_params=_params(dimension_semantics=("parallel", "parallel")),
        name="inproj",
    )(x, ng, wtok, wfeat, gmat, gains, gfeat)


def _steps_per_iteration(nk):
    steps = STEPS_A
    while steps > MIN_STEPS_A and nk < 2 * steps:
        steps //= 2
    return steps


def _diff_attn_kernel(c_ref, lam_ref, trips_ref, qt_ref, k_ref, vt_ref, kbias_ref, dbias_ref, g2_ref, o_ref,
                      kp_ref, qv_ref, sa_ref, sb_ref, ta_ref, tb_ref, m_ref, acc_ref, *, seq):
    bf16, f32 = jnp.bfloat16, jnp.float32
    h = pl.program_id(1)
    qi = pl.program_id(2)
    nk = seq // TK_A
    n_steps = _steps_per_iteration(nk)
    c = c_ref[h]

    @pl.when(qi == 0)
    def _():
        lane = lax.broadcasted_iota(jnp.int32, (TK_A, LANES), 1)
        kb0 = kbias_ref[0, 0].astype(f32)
        kb1 = kbias_ref[0, 1].astype(f32)

        def body(j, carry):
            rows = pl.ds(pl.multiple_of(j * TK_A, TK_A), TK_A)
            kblk = k_ref[0, rows, :].astype(f32)
            kp_ref[0, rows, :] = jnp.where(lane < HEAD_DIM, kblk, kb0).astype(bf16)
            kp_ref[1, rows, :] = jnp.where(lane >= HEAD_DIM, kblk, kb1).astype(bf16)
            return carry

        lax.fori_loop(0, nk, body, 0)

    row = lax.broadcasted_iota(jnp.int32, (2 * HEAD_DIM, TQ_A), 0)
    qblk = qt_ref[0].astype(f32)
    for sub in range(2):
        own = (row < HEAD_DIM) if sub == 0 else (row >= HEAD_DIM)
        b0 = HEAD_DIM if sub == 0 else 0
        brow = (row >= b0) & (row < b0 + N_BIAS_LANES)
        base = jnp.where(own, qblk, 0.0)
        for var, sigma in enumerate((1.0, -1.0, 0.0)):
            qv_ref[3 * sub + var] = jnp.where(brow, sigma, base).astype(bf16)

    m_ref[...] = jnp.full(m_ref.shape, MASKED, f32)
    acc_ref[...] = jnp.zeros(acc_ref.shape, f32)

    i0 = qi * TQ_A
    jd = i0 // TK_A
    di = lax.broadcasted_iota(jnp.int32, (1, TQ_A), 1)

    def update(sub, t, tmax, rvec, vt):
        m_old = m_ref[sub]
        m_new = jnp.maximum(m_old, tmax + rvec)
        alpha = jnp.exp2(m_old - m_new)
        p = jnp.exp2(t - (m_new - rvec))
        acc_ref[sub] = alpha * acc_ref[sub] + jnp.dot(vt, p.astype(bf16), preferred_element_type=f32)
        m_ref[sub] = m_new

    def put_scores(s_ref, t_ref, sub, t):
        s_ref[sub] = t
        t_ref[sub] = jnp.max(t, axis=0, keepdims=True)

    def block_of(step):
        jj = step - 1
        j = jnp.where(step == 0, jd, jj + (jj >= jd).astype(jnp.int32))
        return j, j > jd

    def produce(bufs, step):
        s_ref, t_ref = bufs
        j, after = block_of(step)
        var = after.astype(jnp.int32)
        rows = pl.ds(pl.multiple_of(j * TK_A, TK_A), TK_A)
        for sub in range(2):
            put_scores(s_ref, t_ref, sub,
                       jnp.dot(kp_ref[sub, rows, :], qv_ref[3 * sub + var], preferred_element_type=f32))

    def consume(bufs, step):
        s_ref, t_ref = bufs
        j, after = block_of(step)
        sgn_c = jnp.where(j == jd, 0.0, jnp.where(after, c, -c)).astype(f32)
        rvec = sgn_c * (i0 - j * TK_A + di).astype(f32)
        vt = vt_ref[0, 0, j]
        for sub in range(2):
            update(sub, s_ref[sub], t_ref[sub], rvec, vt)

    buf_a = (sa_ref, ta_ref)
    buf_b = (sb_ref, tb_ref)

    rows_d = pl.ds(pl.multiple_of(jd * TK_A, TK_A), TK_A)
    for sub in range(2):
        put_scores(sa_ref, ta_ref, sub,
                   jnp.dot(kp_ref[sub, rows_d, :], qv_ref[3 * sub + 2], preferred_element_type=f32) + dbias_ref[0])

    def steps(s0, produce_last):
        for k in range(0, n_steps, 2):
            produce(buf_b, s0 + k + 1)
            consume(buf_a, s0 + k)
            if produce_last or k + 2 < n_steps:
                produce(buf_a, s0 + k + 2)
            consume(buf_b, s0 + k + 1)

    def body(i, carry):
        steps(n_steps * i, True)
        return carry

    lax.fori_loop(0, trips_ref[0], body, 0)
    steps(nk - n_steps, False)

    nf = 2 * HEAD_DIM
    o = (acc_ref[0, :nf, :] / acc_ref[0, nf:nf + 1, :]
         - lam_ref[0] * (acc_ref[1, :nf, :] / acc_ref[1, nf:nf + 1, :]))
    y = o * lax.rsqrt(jnp.mean(o * o, axis=0, keepdims=True) + EPS) * g2_ref[...]
    o_ref[0] = y.T.astype(o_ref.dtype)


def _diff_attn(cvec, lam, qat, ka, vat, kbias, dbias, g2):
    B, _, T = qat.shape
    nq = T // TQ_A
    nkb = T // TK_A
    smem = pl.BlockSpec(memory_space=pltpu.SMEM)
    trips = jnp.full((1,), nkb // _steps_per_iteration(nkb) - 1, jnp.int32)
    return pl.pallas_call(
        functools.partial(_diff_attn_kernel, seq=T),
        grid=(B, H_A, nq),
        in_specs=[
            smem, smem, smem,
            pl.BlockSpec((1, 2 * HEAD_DIM, TQ_A), lambda b, h, q: (b, h, q)),
            pl.BlockSpec((1, T, 2 * HEAD_DIM), lambda b, h, q: (b, 0, h)),
            pl.BlockSpec((1, 1, nkb, VT_ROWS, TK_A), lambda b, h, q: (b, h, 0, 0, 0)),
            pl.BlockSpec((1, 2, TK_A, LANES), lambda b, h, q: (h, 0, 0, 0)),
            pl.BlockSpec((1, TK_A, TQ_A), lambda b, h, q: (h, 0, 0)),
            pl.BlockSpec((2 * HEAD_DIM, 1), lambda b, h, q: (0, 0)),
        ],
        out_specs=pl.BlockSpec((1, TQ_A, 2 * HEAD_DIM), lambda b, h, q: (b, q, h)),
        out_shape=jax.ShapeDtypeStruct((B, T, W_A), jnp.bfloat16),
        scratch_shapes=[
            pltpu.VMEM((2, T, LANES), jnp.bfloat16),
            pltpu.VMEM((6, 2 * HEAD_DIM, TQ_A), jnp.bfloat16),
            pltpu.VMEM((2, TK_A, TQ_A), jnp.float32),
            pltpu.VMEM((2, TK_A, TQ_A), jnp.float32),
            pltpu.VMEM((2, 1, TQ_A), jnp.float32),
            pltpu.VMEM((2, 1, TQ_A), jnp.float32),
            pltpu.VMEM((2, 1, TQ_A), jnp.float32),
            pltpu.VMEM((2, VT_ROWS, TQ_A), jnp.float32),
        ],
        compiler_params=_params(dimension_semantics=("parallel", "parallel", "arbitrary")),
        name="diff_attn",
    )(cvec, lam, trips, qat, ka, vat, kbias, dbias, g2)


def _win_attn_kernel(sink_ref, qt_ref, k_ref, vt_ref, bias_ref, o_ref, *, seq, win, lead):
    bf16, f32 = jnp.bfloat16, jnp.float32
    pair = pl.program_id(1)
    step = pl.program_id(2)
    nq = seq // TQ_W
    nf = 2 * HEAD_DIM
    row = lax.broadcasted_iota(jnp.int32, (nf, TQ_W), 0)
    sinks = [sink_ref[2 * pair + hf] for hf in range(2)]

    jobs = [(n, hf) for n in range(QBLOCKS_W) for hf in range(2)]
    kws, vts, variants, qts = [], [], [], []
    for n in range(QBLOCKS_W):
        qi = step * QBLOCKS_W + n
        ws = pl.multiple_of(jnp.clip(qi * TQ_W - lead, 0, seq - win), LANES)
        variants.append(jnp.where(qi == 0, 0, jnp.where(qi == nq - 1, 2, 1)))
        kws.append(k_ref[0, pl.ds(ws, win), :])
        slab = ws // LANES
        vts.append(jnp.concatenate([vt_ref[0, 0, slab + c] for c in range(win // LANES)], axis=1))
        qts.append(qt_ref[0, :, n * TQ_W:(n + 1) * TQ_W].astype(f32))
    qms = [jnp.where((row < HEAD_DIM) if hf == 0 else (row >= HEAD_DIM), qts[n], 0.0).astype(bf16)
           for n, hf in jobs]
    sts = [jnp.dot(kws[n], qms[j], preferred_element_type=f32) + bias_ref[variants[n], hf]
           for j, (n, hf) in enumerate(jobs)]
    ms = [jnp.maximum(jnp.max(sts[j], axis=0, keepdims=True), sinks[hf]) for j, (n, hf) in enumerate(jobs)]
    pts = [jnp.exp2(sts[j] - ms[j]).astype(bf16) for j in range(len(jobs))]
    accs = [jnp.dot(vts[n], pts[j], preferred_element_type=f32) for j, (n, hf) in enumerate(jobs)]
    ots = [accs[j][hf * HEAD_DIM:(hf + 1) * HEAD_DIM] / (accs[j][nf:nf + 1] + jnp.exp2(sinks[hf] - ms[j]))
           for j, (n, hf) in enumerate(jobs)]
    for n in range(QBLOCKS_W):
        ot = jnp.concatenate([ots[2 * n], ots[2 * n + 1]], axis=0)
        o_ref[0, n * TQ_W:(n + 1) * TQ_W, :] = ot.T.astype(o_ref.dtype)


def _win_attn(sink2, qt, k, vt, bias, *, win, lead, kv_group):
    B, width, T = qt.shape
    n_pairs = width // LANES
    tq_step = QBLOCKS_W * TQ_W
    return pl.pallas_call(
        functools.partial(_win_attn_kernel, seq=T, win=win, lead=lead),
        grid=(B, n_pairs, T // tq_step),
        in_specs=[
            pl.BlockSpec(memory_space=pltpu.SMEM),
            pl.BlockSpec((1, LANES, tq_step), lambda b, p, s: (b, p, s)),
            pl.BlockSpec((1, T, LANES), lambda b, p, s: (b, 0, p // kv_group)),
            pl.BlockSpec((1, 1, T // LANES, VT_ROWS, LANES), lambda b, p, s: (b, p // kv_group, 0, 0, 0)),
            pl.BlockSpec((3, 2, win, TQ_W), lambda b, p, s: (0, p, 0, 0)),
        ],
        out_specs=pl.BlockSpec((1, tq_step, LANES), lambda b, p, s: (b, s, p)),
        out_shape=jax.ShapeDtypeStruct((B, T, width), jnp.bfloat16),
        compiler_params=_params(dimension_semantics=("parallel", "parallel", "arbitrary")),
        name="win_attn",
    )(sink2, qt, k, vt, bias)


def _merge_kernel(x_ref, oa_ref, ob_ref, oc_ref, za_ref, zb_ref, zc_ref, g_ref,
                  wa_ref, wb_ref, wc_ref, wo_ref, y_ref):
    bf16, f32 = jnp.bfloat16, jnp.float32

    def sigmoid(v):
        return 0.5 * jnp.tanh(0.5 * v) + 0.5

    def branch(o_ref, z_ref, w_ref, i):
        z = z_ref[0].astype(f32)
        u = o_ref[0].astype(f32) * (z * sigmoid(z))
        y = jnp.dot(u.astype(bf16), w_ref[...], preferred_element_type=f32)
        gate = g_ref[0, :, i * D_MODEL:(i + 1) * D_MODEL].astype(f32)
        return sigmoid(gate) * y

    m = branch(oa_ref, za_ref, wa_ref, 0) + branch(ob_ref, zb_ref, wb_ref, 1) + branch(oc_ref, zc_ref, wc_ref, 2)
    y_ref[0] = x_ref[0] + jnp.dot(m.astype(bf16), wo_ref[...], preferred_element_type=f32)


def _merge(x, oa, ob, oc, za, zb, zc, g, wa, wb, wc, wo):
    B, T, _ = x.shape
    tok = lambda w: pl.BlockSpec((1, TM, w), lambda b, t: (b, t, 0))
    const2 = lambda a: _resident(a.shape, lambda b, t: (0, 0))
    return pl.pallas_call(
        _merge_kernel,
        grid=(B, T // TM),
        in_specs=[tok(D_MODEL), tok(W_A), tok(W_B), tok(W_C), tok(W_A), tok(W_B), tok(W_C), tok(3 * D_MODEL),
                  const2(wa), const2(wb), const2(wc), const2(wo)],
        out_specs=tok(D_MODEL),
        out_shape=jax.ShapeDtypeStruct(x.shape, x.dtype),
        compiler_params=_params(dimension_semantics=("parallel", "parallel")),
        name="merge",
    )(x, oa, ob, oc, za, zb, zc, g, wa, wb, wc, wo)


def _split3_bf16(v):
    hi = v.astype(jnp.bfloat16)
    r1 = v - hi.astype(jnp.float32)
    mid = r1.astype(jnp.bfloat16)
    lo = (r1 - mid.astype(jnp.float32)).astype(jnp.bfloat16)
    return hi, mid, lo


def _diff_constants():
    c = np.asarray(_alibi_slopes(H_A), np.float32) * np.float32(LOG2E)
    col = jnp.asarray(c)[:, None] * jnp.arange(TK_A, dtype=jnp.float32)[None, :]
    pieces = jnp.stack(_split3_bf16(col), axis=-1)
    kbias = jnp.zeros((H_A, 2, TK_A, LANES), jnp.bfloat16)
    kbias = kbias.at[:, 0, :, HEAD_DIM:HEAD_DIM + N_BIAS_LANES].set(pieces)
    kbias = kbias.at[:, 1, :, 0:N_BIAS_LANES].set(pieces)
    dist = np.abs(np.arange(TQ_A)[None, :] - np.arange(TK_A)[:, None]).astype(np.float32)
    dbias = jnp.asarray(-c[:, None, None] * dist[None])
    return jnp.asarray(c), kbias, dbias


def _window_bias_c():
    slopes = np.asarray(_alibi_slopes(H_C), np.float64)
    di = np.arange(TQ_W)[None, :]
    dj = np.arange(WIN_C)[:, None]
    tiles = []
    for lead in (0, WINDOW, WIN_C - TQ_W):
        dist = np.abs(lead + di - dj)
        tile = np.where(dist <= WINDOW, -(slopes[:, None, None] * LOG2E) * dist[None], MASKED)
        tiles.append(tile)
    return jnp.asarray(np.stack(tiles).astype(np.float32))


def _window_bias_b(rpb):
    q_rows = TQ_W // GRID_W
    w_rows = WIN_B // GRID_W
    n_dr, n_dc = 2 * MAX_KH - 1, 2 * KW - 1
    table = jnp.pad(rpb.astype(jnp.float32) * LOG2E, ((0, 0), (0, 1), (0, 1)), constant_values=MASKED)
    qc = np.arange(GRID_W)[:, None]
    kc = np.arange(GRID_W)[None, :]
    cstart = np.clip(qc - KW // 2, 0, GRID_W - KW)
    col_ok = (kc >= cstart) & (kc < cstart + KW)
    dc = np.where(col_ok, kc - qc + (KW - 1), n_dc)
    col_sel = dc[:, :, None] == np.arange(n_dc + 1)
    qr = np.arange(q_rows)[:, None]
    wr = np.arange(w_rows)[None, :]
    row_sel = []
    for lead_rows, rs in ((0, 0 * qr), (MAX_KH // 2, qr), (w_rows - q_rows, 0 * qr + MAX_KH // 2)):
        row_ok = (wr >= rs) & (wr < rs + MAX_KH)
        dr = np.where(row_ok, wr - (lead_rows + qr) + (MAX_KH - 1), n_dr)
        row_sel.append(dr[:, :, None] == np.arange(n_dr + 1))
    row_sel = jnp.asarray(np.stack(row_sel), jnp.float32)
    vals = jnp.einsum("vrwd,hde,cke->vhwkrc", row_sel, table, jnp.asarray(col_sel, jnp.float32),
                      precision=lax.Precision.HIGHEST)
    return vals.reshape(3, H_B, WIN_B, TQ_W)


def _layer_weights(l, w_in, norm_g, qk_gain_a, qk_gain_b, qk_gain_c):
    bf16, f32 = jnp.bfloat16, jnp.float32
    w = w_in[l]
    sizes = (W_A,) * 4 + (W_B,) * 4 + (W_C, KV_C * HEAD_DIM, KV_C * HEAD_DIM, W_C) + (D_MODEL,) * 3
    offs = np.concatenate([[0], np.cumsum(sizes)])
    names = ("qa", "ka", "va", "za", "qb", "kb", "vb", "zb", "qc", "kc", "vc", "zc", "ga", "gb", "gc")
    cols = {n: w[:, offs[i]:offs[i + 1]] for i, n in enumerate(names)}
    dup = lambda a: jnp.concatenate([a[:, :HEAD_DIM], a[:, :HEAD_DIM], a[:, HEAD_DIM:], a[:, HEAD_DIM:]], axis=1)
    cols["kc"] = dup(cols["kc"])
    cols["vc"] = dup(cols["vc"])
    wtok = jnp.concatenate([cols[n] for n, _ in _TOK_SIZES], axis=1).astype(bf16)
    wfeat = jnp.concatenate([cols[n] for n, _ in _FEAT_SIZES], axis=1).T.astype(bf16)
    qscale = QK_SCALE * LOG2E
    tile8 = lambda g: jnp.tile(g.astype(f32), W_A // HEAD_DIM)
    zero = jnp.zeros((W_A,), f32)
    gains = jnp.stack([tile8(qk_gain_a[l, 1]), tile8(qk_gain_b[l, 1]), tile8(qk_gain_c[l, 1])] + [zero] * 5)
    gfeat = jnp.stack([qk_gain_a[l, 0], qk_gain_b[l, 0], qk_gain_c[l, 0]]).astype(f32)[:, :, None] * qscale
    ng = norm_g[l].astype(f32)[None, :]
    return ng, wtok, wfeat, gains, gfeat


def _trunk(x, layers, consts):
    cvec, kbias, dbias, gmat, bias_c, no_sink = consts
    for lw in layers:
        (ng, wtok, wfeat, gains, gfeat, lam, g2, bias_b, sink2, wa, wb, wc, wo) = lw
        (qat, ka, vat, za, qbt, kb, vbt, zb, qct, kc, vct, zc, g) = _inproj(x, ng, wtok, wfeat, gmat, gains, gfeat)
        oa = _diff_attn(cvec, lam, qat, ka, vat, kbias, dbias, g2)
        ob = _win_attn(no_sink, qbt, kb, vbt, bias_b, win=WIN_B, lead=(MAX_KH // 2) * GRID_W, kv_group=1)
        oc = _win_attn(sink2, qct, kc, vct, bias_c, win=WIN_C, lead=WINDOW, kv_group=2)
        x = _merge(x, oa, ob, oc, za, zb, zc, g, wa, wb, wc, wo)
    return x


def kernel(x_prompt, x_sample, norm_g, w_in, qk_gain_a, lambda_a, subln_g_a, qk_gain_b, rpb_b,
           qk_gain_c, sink_c, w_proj_a, w_proj_b, w_proj_c, w_out):
    bf16, f32 = jnp.bfloat16, jnp.float32
    for x in (x_prompt, x_sample):
        T = x.shape[1]
        nk = T // TK_A
        assert T % TK_A == 0 and nk % _steps_per_iteration(nk) == 0
        assert T % (QBLOCKS_W * TQ_W) == 0 and T >= 3 * TQ_W and T >= WIN_B and x.shape[2] == D_MODEL
    cvec, kbias, dbias = _diff_constants()
    gidx = np.arange(NORM_TILE) // HEAD_DIM
    gmat = jnp.asarray((gidx[:, None] == gidx[None, :]).astype(np.float32) / HEAD_DIM, bf16)
    consts = (cvec, kbias, dbias, gmat, _window_bias_c(), jnp.full((H_B,), MASKED, f32))
    layers = []
    for l in range(DEPTH):
        lam_init = 0.8 - 0.6 * math.exp(-0.3 * l)
        lv = lambda_a[l].astype(f32)
        lam = (jnp.exp(jnp.sum(lv[0] * lv[1])) - jnp.exp(jnp.sum(lv[2] * lv[3])) + lam_init).reshape(1)
        g2 = (subln_g_a[l].astype(f32) * (1.0 - lam_init))[:, None]
        layers.append(_layer_weights(l, w_in, norm_g, qk_gain_a, qk_gain_b, qk_gain_c)
                      + (lam, g2, _window_bias_b(rpb_b[l]), sink_c[l].astype(f32) * LOG2E,
                         w_proj_a[l].astype(bf16), w_proj_b[l].astype(bf16), w_proj_c[l].astype(bf16),
                         w_out[l].astype(bf16)))
    return (_trunk(x_prompt, layers, consts), _trunk(x_sample, layers, consts))
```

```python
import functools
import math

import numpy as np
import jax
import jax.numpy as jnp
from jax import lax
from jax.experimental import pallas as pl
from jax.experimental.pallas import tpu as pltpu

D_MODEL = 1024
DEPTH = 2
HEAD_DIM = 64
EPS = 1e-6
H_A = 4
W_A = H_A * 2 * HEAD_DIM
H_B = 8
W_B = H_B * HEAD_DIM
GRID_W = 64
MAX_KH = 8
KW = 16
H_C = 8
KV_C = 2
W_C = H_C * HEAD_DIM
WINDOW = 128

LOG2E = math.log2(math.e)
QK_SCALE = HEAD_DIM ** -0.5
MASKED = -1e30

LANES = 128
TM = 512
TM_MERGE = 1024
NORM_TILE = 256
TQ_A = TM
TK_A = TM
STEPS_A = 16
MIN_STEPS_A = 8
TQ_W = 256
QBLOCKS_W = 16
WIN_B = (MAX_KH + 4) * GRID_W
WIN_C = TQ_W + 2 * WINDOW
N_BIAS_LANES = 3
BF16_ROWS = 16
VT_ROWS = 2 * HEAD_DIM + BF16_ROWS
VMEM_LIMIT_BYTES = 56 * 1024 * 1024


def _offsets(sizes):
    offs, o = {}, 0
    for name, size in sizes:
        offs[name] = (o, o + size)
        o += size
    return offs


_TOK_SIZES = (("ka", W_A), ("za", W_A), ("kb", W_B), ("zb", W_B), ("kc", 4 * HEAD_DIM), ("zc", W_C),
              ("ga", D_MODEL), ("gb", D_MODEL), ("gc", D_MODEL))
_FEAT_SIZES = (("qa", W_A), ("va", W_A), ("qb", W_B), ("vb", W_B), ("qc", W_C), ("vc", 4 * HEAD_DIM))
_TOK_OFF = _offsets(_TOK_SIZES)
_FEAT_OFF = _offsets(_FEAT_SIZES)


def _alibi_slopes(n):
    return [2.0 ** (-8.0 * (i + 1) / n) for i in range(n)]


def _params(**kw):
    return pltpu.CompilerParams(vmem_limit_bytes=VMEM_LIMIT_BYTES, **kw)


def _resident(block_shape, index_map):
    return pl.BlockSpec(block_shape, index_map, pipeline_mode=pl.Buffered(1))


def _inproj_kernel(x_ref, ng_ref, wtok_ref, wfeat_ref, gmat_ref, gains_ref, gfeat_ref,
                   qat_ref, ka_ref, vat_ref, za_ref, qbt_ref, kb_ref, vbt_ref, zb_ref,
                   qct_ref, kc_ref, vct_ref, zc_ref, g_ref):
    bf16, f32 = jnp.bfloat16, jnp.float32
    x = x_ref[0]
    ms = jnp.mean(x * x, axis=-1, keepdims=True)
    h = (x * lax.rsqrt(ms + EPS) * ng_ref[...]).astype(bf16)

    def proj(name):
        c0, c1 = _TOK_OFF[name]
        return jnp.dot(h, wtok_ref[:, c0:c1], preferred_element_type=f32)

    def normed(name, gain_row):
        p = proj(name)
        w = p.shape[-1]
        pp = (p * p).astype(bf16)
        ss = jnp.concatenate([jnp.dot(pp[:, c0:c0 + NORM_TILE], gmat_ref[...], preferred_element_type=f32)
                              for c0 in range(0, w, NORM_TILE)], axis=1)
        return (p * lax.rsqrt(ss + EPS) * gains_ref[gain_row:gain_row + 1, :w]).astype(bf16)

    ka_ref[0] = normed("ka", 0)
    za_ref[0] = proj("za").astype(bf16)
    kb_ref[0] = normed("kb", 1)
    zb_ref[0] = proj("zb").astype(bf16)
    kc_ref[0] = normed("kc", 2)
    zc_ref[0] = proj("zc").astype(bf16)
    for i, name in enumerate(("ga", "gb", "gc")):
        g_ref[0, :, i * D_MODEL:(i + 1) * D_MODEL] = proj(name).astype(bf16)

    def proj_t(name):
        r0, r1 = _FEAT_OFF[name]
        return lax.dot_general(wfeat_ref[r0:r1, :], h, (((1,), (1,)), ((), ())), preferred_element_type=f32)

    def put_q(name, gain_idx, out_ref):
        pf = proj_t(name)
        for g in range(pf.shape[0] // HEAD_DIM):
            blk = pf[g * HEAD_DIM:(g + 1) * HEAD_DIM]
            r = lax.rsqrt(jnp.mean(blk * blk, axis=0, keepdims=True) + EPS)
            out_ref[0, g * HEAD_DIM:(g + 1) * HEAD_DIM, :] = (blk * r * gfeat_ref[gain_idx]).astype(bf16)

    put_q("qa", 0, qat_ref)
    put_q("qb", 1, qbt_ref)
    put_q("qc", 2, qct_ref)

    pad_row = lax.broadcasted_iota(jnp.int32, (BF16_ROWS, TM), 0)
    ones_row = jnp.where(pad_row == 0, 1.0, 0.0).astype(bf16)
    nf = 2 * HEAD_DIM
    pf = proj_t("va")
    for hh in range(H_A):
        vat_ref[0, hh, 0, :nf, :] = pf[hh * nf:(hh + 1) * nf].astype(bf16)
        vat_ref[0, hh, 0, nf:, :] = ones_row
    for name, out_ref in (("vb", vbt_ref), ("vc", vct_ref)):
        pf = proj_t(name)
        for p in range(pf.shape[0] // nf):
            for s in range(TM // LANES):
                out_ref[0, p, s, :nf, :] = pf[p * nf:(p + 1) * nf, s * LANES:(s + 1) * LANES].astype(bf16)
                out_ref[0, p, s, nf:, :] = ones_row[:, :LANES]


def _inproj(x, ng, wtok, wfeat, gmat, gains, gfeat):
    B, T, _ = x.shape
    nt = T // TM
    bf16 = jnp.bfloat16
    tok = lambda w: pl.BlockSpec((1, TM, w), lambda b, t: (b, t, 0))
    feat = lambda w: pl.BlockSpec((1, w, TM), lambda b, t: (b, 0, t))
    slabs = TM // LANES
    vt_slabs = lambda n: pl.BlockSpec((1, n, slabs, VT_ROWS, LANES), lambda b, t: (b, 0, t, 0, 0))
    const = lambda a: _resident(a.shape, lambda b, t: (0,) * a.ndim)
    out_shape = (
        jax.ShapeDtypeStruct((B, W_A, T), bf16),
        jax.ShapeDtypeStruct((B, T, W_A), bf16),
        jax.ShapeDtypeStruct((B, H_A, nt, VT_ROWS, TM), bf16),
        jax.ShapeDtypeStruct((B, T, W_A), bf16),
        jax.ShapeDtypeStruct((B, W_B, T), bf16),
        jax.ShapeDtypeStruct((B, T, W_B), bf16),
        jax.ShapeDtypeStruct((B, H_B // 2, T // LANES, VT_ROWS, LANES), bf16),
        jax.ShapeDtypeStruct((B, T, W_B), bf16),
        jax.ShapeDtypeStruct((B, W_C, T), bf16),
        jax.ShapeDtypeStruct((B, T, 4 * HEAD_DIM), bf16),
        jax.ShapeDtypeStruct((B, KV_C, T // LANES, VT_ROWS, LANES), bf16),
        jax.ShapeDtypeStruct((B, T, W_C), bf16),
        jax.ShapeDtypeStruct((B, T, 3 * D_MODEL), bf16),
    )
    out_specs = (
        feat(W_A),
        tok(W_A),
        pl.BlockSpec((1, H_A, 1, VT_ROWS, TM), lambda b, t: (b, 0, t, 0, 0)),
        tok(W_A), feat(W_B), tok(W_B), vt_slabs(H_B // 2), tok(W_B), feat(W_C),
        tok(4 * HEAD_DIM), vt_slabs(KV_C), tok(W_C), tok(3 * D_MODEL),
    )
    return pl.pallas_call(
        _inproj_kernel,
        grid=(B, nt),
        in_specs=[pl.BlockSpec((1, TM, D_MODEL), lambda b, t: (b, t, 0)),
                  const(ng), const(wtok), const(wfeat), const(gmat), const(gains), const(gfeat)],
        out_specs=out_specs,
        out_shape=out_shape,
        compiler_params=_params(dimension_semantics=("parallel", "parallel")),
        name="inproj",
    )(x, ng, wtok, wfeat, gmat, gains, gfeat)


def _steps_per_iteration(nk):
    steps = STEPS_A
    while steps > MIN_STEPS_A and nk < 2 * steps:
        steps //= 2
    return steps


def _diff_attn_kernel(c_ref, lam_ref, trips_ref, qt_ref, k_ref, vt_ref, kbias_ref, dbias_ref, g2_ref, o_ref,
                      kp_ref, qv_ref, sa_ref, sb_ref, ta_ref, tb_ref, m_ref, acc_ref, *, seq):
    bf16, f32 = jnp.bfloat16, jnp.float32
    h = pl.program_id(1)
    qi = pl.program_id(2)
    nk = seq // TK_A
    n_steps = _steps_per_iteration(nk)
    c = c_ref[h]

    @pl.when(qi == 0)
    def _():
        lane = lax.broadcasted_iota(jnp.int32, (TK_A, LANES), 1)
        kb0 = kbias_ref[0, 0].astype(f32)
        kb1 = kbias_ref[0, 1].astype(f32)

        def body(j, carry):
            rows = pl.ds(pl.multiple_of(j * TK_A, TK_A), TK_A)
            kblk = k_ref[0, rows, :].astype(f32)
            kp_ref[0, rows, :] = jnp.where(lane < HEAD_DIM, kblk, kb0).astype(bf16)
            kp_ref[1, rows, :] = jnp.where(lane >= HEAD_DIM, kblk, kb1).astype(bf16)
            return carry

        lax.fori_loop(0, nk, body, 0)

    row = lax.broadcasted_iota(jnp.int32, (2 * HEAD_DIM, TQ_A), 0)
    qblk = qt_ref[0].astype(f32)
    for sub in range(2):
        own = (row < HEAD_DIM) if sub == 0 else (row >= HEAD_DIM)
        b0 = HEAD_DIM if sub == 0 else 0
        brow = (row >= b0) & (row < b0 + N_BIAS_LANES)
        base = jnp.where(own, qblk, 0.0)
        for var, sigma in enumerate((1.0, -1.0, 0.0)):
            qv_ref[3 * sub + var] = jnp.where(brow, sigma, base).astype(bf16)

    m_ref[...] = jnp.full(m_ref.shape, MASKED, f32)
    acc_ref[...] = jnp.zeros(acc_ref.shape, f32)

    i0 = qi * TQ_A
    jd = i0 // TK_A
    di = lax.broadcasted_iota(jnp.int32, (1, TQ_A), 1)

    def update(sub, t, tmax, rvec, vt):
        m_old = m_ref[sub]
        m_new = jnp.maximum(m_old, tmax + rvec)
        alpha = jnp.exp2(m_old - m_new)
        p = jnp.exp2(t - (m_new - rvec))
        acc_ref[sub] = alpha * acc_ref[sub] + jnp.dot(vt, p.astype(bf16), preferred_element_type=f32)
        m_ref[sub] = m_new

    def put_scores(s_ref, t_ref, sub, t):
        s_ref[sub] = t
        t_ref[sub] = jnp.max(t, axis=0, keepdims=True)

    def block_of(step):
        jj = step - 1
        j = jnp.where(step == 0, jd, jj + (jj >= jd).astype(jnp.int32))
        return j, j > jd

    def produce(bufs, step):
        s_ref, t_ref = bufs
        j, after = block_of(step)
        var = after.astype(jnp.int32)
        rows = pl.ds(pl.multiple_of(j * TK_A, TK_A), TK_A)
        for sub in range(2):
            put_scores(s_ref, t_ref, sub,
                       jnp.dot(kp_ref[sub, rows, :], qv_ref[3 * sub + var], preferred_element_type=f32))

    def consume(bufs, step):
        s_ref, t_ref = bufs
        j, after = block_of(step)
        sgn_c = jnp.where(j == jd, 0.0, jnp.where(after, c, -c)).astype(f32)
        rvec = sgn_c * (i0 - j * TK_A + di).astype(f32)
        vt = vt_ref[0, 0, j]
        for sub in range(2):
            update(sub, s_ref[sub], t_ref[sub], rvec, vt)

    buf_a = (sa_ref, ta_ref)
    buf_b = (sb_ref, tb_ref)

    rows_d = pl.ds(pl.multiple_of(jd * TK_A, TK_A), TK_A)
    for sub in range(2):
        put_scores(sa_ref, ta_ref, sub,
                   jnp.dot(kp_ref[sub, rows_d, :], qv_ref[3 * sub + 2], preferred_element_type=f32) + dbias_ref[0])

    def steps(s0, produce_last):
        for k in range(0, n_steps, 2):
            produce(buf_b, s0 + k + 1)
            consume(buf_a, s0 + k)
            if produce_last or k + 2 < n_steps:
                produce(buf_a, s0 + k + 2)
            consume(buf_b, s0 + k + 1)

    def body(i, carry):
        steps(n_steps * i, True)
        return carry

    lax.fori_loop(0, trips_ref[0], body, 0)
    steps(nk - n_steps, False)

    nf = 2 * HEAD_DIM
    o = (acc_ref[0, :nf, :] / acc_ref[0, nf:nf + 1, :]
         - lam_ref[0] * (acc_ref[1, :nf, :] / acc_ref[1, nf:nf + 1, :]))
    y = o * lax.rsqrt(jnp.mean(o * o, axis=0, keepdims=True) + EPS) * g2_ref[...]
    o_ref[0] = y.T.astype(o_ref.dtype)


def _diff_attn(cvec, lam, qat, ka, vat, kbias, dbias, g2):
    B, _, T = qat.shape
    nq = T // TQ_A
    nkb = T // TK_A
    smem = pl.BlockSpec(memory_space=pltpu.SMEM)
    trips = jnp.full((1,), nkb // _steps_per_iteration(nkb) - 1, jnp.int32)
    return pl.pallas_call(
        functools.partial(_diff_attn_kernel, seq=T),
        grid=(B, H_A, nq),
        in_specs=[
            smem, smem, smem,
            pl.BlockSpec((1, 2 * HEAD_DIM, TQ_A), lambda b, h, q: (b, h, q)),
            pl.BlockSpec((1, T, 2 * HEAD_DIM), lambda b, h, q: (b, 0, h)),
            pl.BlockSpec((1, 1, nkb, VT_ROWS, TK_A), lambda b, h, q: (b, h, 0, 0, 0)),
            pl.BlockSpec((1, 2, TK_A, LANES), lambda b, h, q: (h, 0, 0, 0)),
            pl.BlockSpec((1, TK_A, TQ_A), lambda b, h, q: (h, 0, 0)),
            pl.BlockSpec((2 * HEAD_DIM, 1), lambda b, h, q: (0, 0)),
        ],
        out_specs=pl.BlockSpec((1, TQ_A, 2 * HEAD_DIM), lambda b, h, q: (b, q, h)),
        out_shape=jax.ShapeDtypeStruct((B, T, W_A), jnp.bfloat16),
        scratch_shapes=[
            pltpu.VMEM((2, T, LANES), jnp.bfloat16),
            pltpu.VMEM((6, 2 * HEAD_DIM, TQ_A), jnp.bfloat16),
            pltpu.VMEM((2, TK_A, TQ_A), jnp.float32),
            pltpu.VMEM((2, TK_A, TQ_A), jnp.float32),
            pltpu.VMEM((2, 1, TQ_A), jnp.float32),
            pltpu.VMEM((2, 1, TQ_A), jnp.float32),
            pltpu.VMEM((2, 1, TQ_A), jnp.float32),
            pltpu.VMEM((2, VT_ROWS, TQ_A), jnp.float32),
        ],
        compiler_params=_params(dimension_semantics=("parallel", "parallel", "arbitrary")),
        name="diff_attn",
    )(cvec, lam, trips, qat, ka, vat, kbias, dbias, g2)


def _win_attn_kernel(sink_ref, qt_ref, k_ref, vt_ref, bias_ref, o_ref, *, seq, win, lead):
    bf16, f32 = jnp.bfloat16, jnp.float32
    pair = pl.program_id(1)
    step = pl.program_id(2)
    nq = seq // TQ_W
    nf = 2 * HEAD_DIM
    row = lax.broadcasted_iota(jnp.int32, (nf, TQ_W), 0)
    sinks = [sink_ref[2 * pair + hf] for hf in range(2)]

    jobs = [(n, hf) for n in range(QBLOCKS_W) for hf in range(2)]
    kws, vts, variants, qts = [], [], [], []
    for n in range(QBLOCKS_W):
        qi = step * QBLOCKS_W + n
        ws = pl.multiple_of(jnp.clip(qi * TQ_W - lead, 0, seq - win), LANES)
        variants.append(jnp.where(qi == 0, 0, jnp.where(qi == nq - 1, 2, 1)))
        kws.append(k_ref[0, pl.ds(ws, win), :])
        slab = ws // LANES
        vts.append(jnp.concatenate([vt_ref[0, 0, slab + c] for c in range(win // LANES)], axis=1))
        qts.append(qt_ref[0, :, n * TQ_W:(n + 1) * TQ_W].astype(f32))
    qms = [jnp.where((row < HEAD_DIM) if hf == 0 else (row >= HEAD_DIM), qts[n], 0.0).astype(bf16)
           for n, hf in jobs]
    sts = [jnp.dot(kws[n], qms[j], preferred_element_type=f32) + bias_ref[variants[n], hf]
           for j, (n, hf) in enumerate(jobs)]
    ms = [jnp.maximum(jnp.max(sts[j], axis=0, keepdims=True), sinks[hf]) for j, (n, hf) in enumerate(jobs)]
    pts = [jnp.exp2(sts[j] - ms[j]).astype(bf16) for j in range(len(jobs))]
    accs = [jnp.dot(vts[n], pts[j], preferred_element_type=f32) for j, (n, hf) in enumerate(jobs)]
    ots = [accs[j][hf * HEAD_DIM:(hf + 1) * HEAD_DIM] / (accs[j][nf:nf + 1] + jnp.exp2(sinks[hf] - ms[j]))
           for j, (n, hf) in enumerate(jobs)]
    for n in range(QBLOCKS_W):
        ot = jnp.concatenate([ots[2 * n], ots[2 * n + 1]], axis=0)
        o_ref[0, n * TQ_W:(n + 1) * TQ_W, :] = ot.T.astype(o_ref.dtype)


def _win_attn(sink2, qt, k, vt, bias, *, win, lead, kv_group):
    B, width, T = qt.shape
    n_pairs = width // LANES
    tq_step = QBLOCKS_W * TQ_W
    return pl.pallas_call(
        functools.partial(_win_attn_kernel, seq=T, win=win, lead=lead),
        grid=(B, n_pairs, T // tq_step),
        in_specs=[
            pl.BlockSpec(memory_space=pltpu.SMEM),
            pl.BlockSpec((1, LANES, tq_step), lambda b, p, s: (b, p, s)),
            pl.BlockSpec((1, T, LANES), lambda b, p, s: (b, 0, p // kv_group)),
            pl.BlockSpec((1, 1, T // LANES, VT_ROWS, LANES), lambda b, p, s: (b, p // kv_group, 0, 0, 0)),
            pl.BlockSpec((3, 2, win, TQ_W), lambda b, p, s: (0, p, 0, 0)),
        ],
        out_specs=pl.BlockSpec((1, tq_step, LANES), lambda b, p, s: (b, s, p)),
        out_shape=jax.ShapeDtypeStruct((B, T, width), jnp.bfloat16),
        compiler_params=_params(dimension_semantics=("parallel", "parallel", "arbitrary")),
        name="win_attn",
    )(sink2, qt, k, vt, bias)


def _merge_kernel(x_ref, oa_ref, ob_ref, oc_ref, za_ref, zb_ref, zc_ref, g_ref,
                  wa_ref, wb_ref, wc_ref, wo_ref, y_ref):
    bf16, f32 = jnp.bfloat16, jnp.float32

    def sigmoid(v):
        return 0.5 * jnp.tanh(0.5 * v) + 0.5

    def branch(o_ref, z_ref, w_ref, i):
        z = z_ref[0].astype(f32)
        u = o_ref[0].astype(f32) * (z * sigmoid(z))
        y = jnp.dot(u.astype(bf16), w_ref[...], preferred_element_type=f32)
        gate = g_ref[0, :, i * D_MODEL:(i + 1) * D_MODEL].astype(f32)
        return sigmoid(gate) * y

    m = branch(oa_ref, za_ref, wa_ref, 0) + branch(ob_ref, zb_ref, wb_ref, 1) + branch(oc_ref, zc_ref, wc_ref, 2)
    y_ref[0] = x_ref[0] + jnp.dot(m.astype(bf16), wo_ref[...], preferred_element_type=f32)


def _merge(x, oa, ob, oc, za, zb, zc, g, wa, wb, wc, wo):
    B, T, _ = x.shape
    tok = lambda w: pl.BlockSpec((1, TM_MERGE, w), lambda b, t: (b, t, 0))
    const2 = lambda a: _resident(a.shape, lambda b, t: (0, 0))
    return pl.pallas_call(
        _merge_kernel,
        grid=(B, T // TM_MERGE),
        in_specs=[tok(D_MODEL), tok(W_A), tok(W_B), tok(W_C), tok(W_A), tok(W_B), tok(W_C), tok(3 * D_MODEL),
                  const2(wa), const2(wb), const2(wc), const2(wo)],
        out_specs=tok(D_MODEL),
        out_shape=jax.ShapeDtypeStruct(x.shape, x.dtype),
        compiler_params=_params(dimension_semantics=("parallel", "parallel")),
        name="merge",
    )(x, oa, ob, oc, za, zb, zc, g, wa, wb, wc, wo)


def _split3_bf16(v):
    hi = v.astype(jnp.bfloat16)
    r1 = v - hi.astype(jnp.float32)
    mid = r1.astype(jnp.bfloat16)
    lo = (r1 - mid.astype(jnp.float32)).astype(jnp.bfloat16)
    return hi, mid, lo


def _diff_constants():
    c = np.asarray(_alibi_slopes(H_A), np.float32) * np.float32(LOG2E)
    col = jnp.asarray(c)[:, None] * jnp.arange(TK_A, dtype=jnp.float32)[None, :]
    pieces = jnp.stack(_split3_bf16(col), axis=-1)
    kbias = jnp.zeros((H_A, 2, TK_A, LANES), jnp.bfloat16)
    kbias = kbias.at[:, 0, :, HEAD_DIM:HEAD_DIM + N_BIAS_LANES].set(pieces)
    kbias = kbias.at[:, 1, :, 0:N_BIAS_LANES].set(pieces)
    dist = np.abs(np.arange(TQ_A)[None, :] - np.arange(TK_A)[:, None]).astype(np.float32)
    dbias = jnp.asarray(-c[:, None, None] * dist[None])
    return jnp.asarray(c), kbias, dbias


def _window_bias_c():
    slopes = np.asarray(_alibi_slopes(H_C), np.float64)
    di = np.arange(TQ_W)[None, :]
    dj = np.arange(WIN_C)[:, None]
    tiles = []
    for lead in (0, WINDOW, WIN_C - TQ_W):
        dist = np.abs(lead + di - dj)
        tile = np.where(dist <= WINDOW, -(slopes[:, None, None] * LOG2E) * dist[None], MASKED)
        tiles.append(tile)
    return jnp.asarray(np.stack(tiles).astype(np.float32))


def _window_bias_b(rpb):
    q_rows = TQ_W // GRID_W
    w_rows = WIN_B // GRID_W
    n_dr, n_dc = 2 * MAX_KH - 1, 2 * KW - 1
    table = jnp.pad(rpb.astype(jnp.float32) * LOG2E, ((0, 0), (0, 1), (0, 1)), constant_values=MASKED)
    qc = np.arange(GRID_W)[:, None]
    kc = np.arange(GRID_W)[None, :]
    cstart = np.clip(qc - KW // 2, 0, GRID_W - KW)
    col_ok = (kc >= cstart) & (kc < cstart + KW)
    dc = np.where(col_ok, kc - qc + (KW - 1), n_dc)
    col_sel = dc[:, :, None] == np.arange(n_dc + 1)
    qr = np.arange(q_rows)[:, None]
    wr = np.arange(w_rows)[None, :]
    row_sel = []
    for lead_rows, rs in ((0, 0 * qr), (MAX_KH // 2, qr), (w_rows - q_rows, 0 * qr + MAX_KH // 2)):
        row_ok = (wr >= rs) & (wr < rs + MAX_KH)
        dr = np.where(row_ok, wr - (lead_rows + qr) + (MAX_KH - 1), n_dr)
        row_sel.append(dr[:, :, None] == np.arange(n_dr + 1))
    row_sel = jnp.asarray(np.stack(row_sel), jnp.float32)
    vals = jnp.einsum("vrwd,hde,cke->vhwkrc", row_sel, table, jnp.asarray(col_sel, jnp.float32),
                      precision=lax.Precision.HIGHEST)
    return vals.reshape(3, H_B, WIN_B, TQ_W)


def _layer_weights(l, w_in, norm_g, qk_gain_a, qk_gain_b, qk_gain_c):
    bf16, f32 = jnp.bfloat16, jnp.float32
    w = w_in[l]
    sizes = (W_A,) * 4 + (W_B,) * 4 + (W_C, KV_C * HEAD_DIM, KV_C * HEAD_DIM, W_C) + (D_MODEL,) * 3
    offs = np.concatenate([[0], np.cumsum(sizes)])
    names = ("qa", "ka", "va", "za", "qb", "kb", "vb", "zb", "qc", "kc", "vc", "zc", "ga", "gb", "gc")
    cols = {n: w[:, offs[i]:offs[i + 1]] for i, n in enumerate(names)}
    dup = lambda a: jnp.concatenate([a[:, :HEAD_DIM], a[:, :HEAD_DIM], a[:, HEAD_DIM:], a[:, HEAD_DIM:]], axis=1)
    cols["kc"] = dup(cols["kc"])
    cols["vc"] = dup(cols["vc"])
    wtok = jnp.concatenate([cols[n] for n, _ in _TOK_SIZES], axis=1).astype(bf16)
    wfeat = jnp.concatenate([cols[n] for n, _ in _FEAT_SIZES], axis=1).T.astype(bf16)
    qscale = QK_SCALE * LOG2E
    tile8 = lambda g: jnp.tile(g.astype(f32), W_A // HEAD_DIM)
    zero = jnp.zeros((W_A,), f32)
    gains = jnp.stack([tile8(qk_gain_a[l, 1]), tile8(qk_gain_b[l, 1]), tile8(qk_gain_c[l, 1])] + [zero] * 5)
    gfeat = jnp.stack([qk_gain_a[l, 0], qk_gain_b[l, 0], qk_gain_c[l, 0]]).astype(f32)[:, :, None] * qscale
    ng = norm_g[l].astype(f32)[None, :]
    return ng, wtok, wfeat, gains, gfeat


def _trunk(x, layers, consts):
    cvec, kbias, dbias, gmat, bias_c, no_sink = consts
    for lw in layers:
        (ng, wtok, wfeat, gains, gfeat, lam, g2, bias_b, sink2, wa, wb, wc, wo) = lw
        (qat, ka, vat, za, qbt, kb, vbt, zb, qct, kc, vct, zc, g) = _inproj(x, ng, wtok, wfeat, gmat, gains, gfeat)
        oa = _diff_attn(cvec, lam, qat, ka, vat, kbias, dbias, g2)
        ob = _win_attn(no_sink, qbt, kb, vbt, bias_b, win=WIN_B, lead=(MAX_KH // 2) * GRID_W, kv_group=1)
        oc = _win_attn(sink2, qct, kc, vct, bias_c, win=WIN_C, lead=WINDOW, kv_group=2)
        x = _merge(x, oa, ob, oc, za, zb, zc, g, wa, wb, wc, wo)
    return x


def kernel(x_prompt, x_sample, norm_g, w_in, qk_gain_a, lambda_a, subln_g_a, qk_gain_b, rpb_b,
           qk_gain_c, sink_c, w_proj_a, w_proj_b, w_proj_c, w_out):
    bf16, f32 = jnp.bfloat16, jnp.float32
    for x in (x_prompt, x_sample):
        T = x.shape[1]
        nk = T // TK_A
        assert T % TK_A == 0 and nk % _steps_per_iteration(nk) == 0
        assert T % (QBLOCKS_W * TQ_W) == 0 and T >= 3 * TQ_W and T >= WIN_B and x.shape[2] == D_MODEL
    cvec, kbias, dbias = _diff_constants()
    gidx = np.arange(NORM_TILE) // HEAD_DIM
    gmat = jnp.asarray((gidx[:, None] == gidx[None, :]).astype(np.float32) / HEAD_DIM, bf16)
    consts = (cvec, kbias, dbias, gmat, _window_bias_c(), jnp.full((H_B,), MASKED, f32))
    layers = []
    for l in range(DEPTH):
        lam_init = 0.8 - 0.6 * math.exp(-0.3 * l)
        lv = lambda_a[l].astype(f32)
        lam = (jnp.exp(jnp.sum(lv[0] * lv[1])) - jnp.exp(jnp.sum(lv[2] * lv[3])) + lam_init).reshape(1)
        g2 = (subln_g_a[l].astype(f32) * (1.0 - lam_init))[:, None]
        layers.append(_layer_weights(l, w_in, norm_g, qk_gain_a, qk_gain_b, qk_gain_c)
                      + (lam, g2, _window_bias_b(rpb_b[l]), sink_c[l].astype(f32) * LOG2E,
                         w_proj_a[l].astype(bf16), w_proj_b[l].astype(bf16), w_proj_c[l].astype(bf16),
                         w_out[l].astype(bf16)))
    return (_trunk(x_prompt, layers, consts), _trunk(x_sample, layers, consts))
```

```python
import functools
import math

import numpy as np
import jax
import jax.numpy as jnp
from jax import lax
from jax.experimental import pallas as pl
from jax.experimental.pallas import tpu as pltpu

D_MODEL = 1024
DEPTH = 2
HEAD_DIM = 64
EPS = 1e-6
H_A = 4
W_A = H_A * 2 * HEAD_DIM
H_B = 8
W_B = H_B * HEAD_DIM
GRID_W = 64
MAX_KH = 8
KW = 16
H_C = 8
KV_C = 2
W_C = H_C * HEAD_DIM
WINDOW = 128

LOG2E = math.log2(math.e)
QK_SCALE = HEAD_DIM ** -0.5
MASKED = -1e30

LANES = 128
TM = 512
TM_MERGE = 1024
NORM_TILE = 256
TQ_A = TM
TK_A = TM
STEPS_A = 16
MIN_STEPS_A = 8
TQ_W = 256
QBLOCKS_W = 16
WIN_B = (MAX_KH + 4) * GRID_W
WIN_C = TQ_W + 2 * WINDOW
N_BIAS_LANES = 3
BF16_ROWS = 16
VT_ROWS = 2 * HEAD_DIM + BF16_ROWS
VMEM_LIMIT_BYTES = 56 * 1024 * 1024


def _offsets(sizes):
    offs, o = {}, 0
    for name, size in sizes:
        offs[name] = (o, o + size)
        o += size
    return offs


_TOK_SIZES = (("ka", W_A), ("za", W_A), ("kb", W_B), ("zb", W_B), ("kc", 4 * HEAD_DIM), ("zc", W_C),
              ("ga", D_MODEL), ("gb", D_MODEL), ("gc", D_MODEL))
_FEAT_SIZES = (("qa", W_A), ("va", W_A), ("qb", W_B), ("vb", W_B), ("qc", W_C), ("vc", 4 * HEAD_DIM))
_TOK_OFF = _offsets(_TOK_SIZES)
_FEAT_OFF = _offsets(_FEAT_SIZES)


def _alibi_slopes(n):
    return [2.0 ** (-8.0 * (i + 1) / n) for i in range(n)]


def _params(**kw):
    return pltpu.CompilerParams(vmem_limit_bytes=VMEM_LIMIT_BYTES, **kw)


def _resident(block_shape, index_map):
    return pl.BlockSpec(block_shape, index_map, pipeline_mode=pl.Buffered(1))


def _inproj_kernel(x_ref, ng_ref, wtok_ref, wfeat_ref, gmat_ref, gains_ref, gfeat_ref,
                   qat_ref, ka_ref, vat_ref, za_ref, qbt_ref, kb_ref, vbt_ref, zb_ref,
                   qct_ref, kc_ref, vct_ref, zc_ref, g_ref):
    bf16, f32 = jnp.bfloat16, jnp.float32
    x = x_ref[0]
    ms = jnp.mean(x * x, axis=-1, keepdims=True)
    h = (x * lax.rsqrt(ms + EPS) * ng_ref[...]).astype(bf16)

    def proj(name):
        c0, c1 = _TOK_OFF[name]
        return jnp.dot(h, wtok_ref[:, c0:c1], preferred_element_type=f32)

    def normed(name, gain_row):
        p = proj(name)
        w = p.shape[-1]
        pp = (p * p).astype(bf16)
        ss = jnp.concatenate([jnp.dot(pp[:, c0:c0 + NORM_TILE], gmat_ref[...], preferred_element_type=f32)
                              for c0 in range(0, w, NORM_TILE)], axis=1)
        return (p * lax.rsqrt(ss + EPS) * gains_ref[gain_row:gain_row + 1, :w]).astype(bf16)

    ka_ref[0] = normed("ka", 0)
    za_ref[0] = proj("za").astype(bf16)
    kb_ref[0] = normed("kb", 1)
    zb_ref[0] = proj("zb").astype(bf16)
    kc_ref[0] = normed("kc", 2)
    zc_ref[0] = proj("zc").astype(bf16)
    for i, name in enumerate(("ga", "gb", "gc")):
        g_ref[0, :, i * D_MODEL:(i + 1) * D_MODEL] = proj(name).astype(bf16)

    def proj_t(name):
        r0, r1 = _FEAT_OFF[name]
        return lax.dot_general(wfeat_ref[r0:r1, :], h, (((1,), (1,)), ((), ())), preferred_element_type=f32)

    def put_q(name, gain_idx, out_ref):
        pf = proj_t(name)
        for g in range(pf.shape[0] // HEAD_DIM):
            blk = pf[g * HEAD_DIM:(g + 1) * HEAD_DIM]
            r = lax.rsqrt(jnp.mean(blk * blk, axis=0, keepdims=True) + EPS)
            out_ref[0, g * HEAD_DIM:(g + 1) * HEAD_DIM, :] = (blk * r * gfeat_ref[gain_idx]).astype(bf16)

    put_q("qa", 0, qat_ref)
    put_q("qb", 1, qbt_ref)
    put_q("qc", 2, qct_ref)

    pad_row = lax.broadcasted_iota(jnp.int32, (BF16_ROWS, TM), 0)
    ones_row = jnp.where(pad_row == 0, 1.0, 0.0).astype(bf16)
    nf = 2 * HEAD_DIM
    pf = proj_t("va")
    for hh in range(H_A):
        vat_ref[0, hh, 0, :nf, :] = pf[hh * nf:(hh + 1) * nf].astype(bf16)
        vat_ref[0, hh, 0, nf:, :] = ones_row
    for name, out_ref in (("vb", vbt_ref), ("vc", vct_ref)):
        pf = proj_t(name)
        for p in range(pf.shape[0] // nf):
            for s in range(TM // LANES):
                out_ref[0, p, s, :nf, :] = pf[p * nf:(p + 1) * nf, s * LANES:(s + 1) * LANES].astype(bf16)
                out_ref[0, p, s, nf:, :] = ones_row[:, :LANES]


def _inproj(x, ng, wtok, wfeat, gmat, gains, gfeat):
    B, T, _ = x.shape
    nt = T // TM
    bf16 = jnp.bfloat16
    tok = lambda w: pl.BlockSpec((1, TM, w), lambda b, t: (b, t, 0))
    feat = lambda w: pl.BlockSpec((1, w, TM), lambda b, t: (b, 0, t))
    slabs = TM // LANES
    vt_slabs = lambda n: pl.BlockSpec((1, n, slabs, VT_ROWS, LANES), lambda b, t: (b, 0, t, 0, 0))
    const = lambda a: _resident(a.shape, lambda b, t: (0,) * a.ndim)
    out_shape = (
        jax.ShapeDtypeStruct((B, W_A, T), bf16),
        jax.ShapeDtypeStruct((B, T, W_A), bf16),
        jax.ShapeDtypeStruct((B, H_A, nt, VT_ROWS, TM), bf16),
        jax.ShapeDtypeStruct((B, T, W_A), bf16),
        jax.ShapeDtypeStruct((B, W_B, T), bf16),
        jax.ShapeDtypeStruct((B, T, W_B), bf16),
        jax.ShapeDtypeStruct((B, H_B // 2, T // LANES, VT_ROWS, LANES), bf16),
        jax.ShapeDtypeStruct((B, T, W_B), bf16),
        jax.ShapeDtypeStruct((B, W_C, T), bf16),
        jax.ShapeDtypeStruct((B, T, 4 * HEAD_DIM), bf16),
        jax.ShapeDtypeStruct((B, KV_C, T // LANES, VT_ROWS, LANES), bf16),
        jax.ShapeDtypeStruct((B, T, W_C), bf16),
        jax.ShapeDtypeStruct((B, T, 3 * D_MODEL), bf16),
    )
    out_specs = (
        feat(W_A),
        tok(W_A),
        pl.BlockSpec((1, H_A, 1, VT_ROWS, TM), lambda b, t: (b, 0, t, 0, 0)),
        tok(W_A), feat(W_B), tok(W_B), vt_slabs(H_B // 2), tok(W_B), feat(W_C),
        tok(4 * HEAD_DIM), vt_slabs(KV_C), tok(W_C), tok(3 * D_MODEL),
    )
    return pl.pallas_call(
        _inproj_kernel,
        grid=(B, nt),
        in_specs=[pl.BlockSpec((1, TM, D_MODEL), lambda b, t: (b, t, 0)),
                  const(ng), const(wtok), const(wfeat), const(gmat), const(gains), const(gfeat)],
        out_specs=out_specs,
        out_shape=out_shape,
        compiler_params=_params(dimension_semantics=("parallel", "parallel")),
        name="inproj",
    )(x, ng, wtok, wfeat, gmat, gains, gfeat)


def _steps_per_iteration(nk):
    steps = STEPS_A
    while steps > MIN_STEPS_A and nk < 2 * steps:
        steps //= 2
    return steps


def _diff_attn_kernel(c_ref, lam_ref, trips_ref, qt_ref, k_ref, vt_ref, kbias_ref, dbias_ref, g2_ref, o_ref,
                      kp_ref, qv_ref, sa_ref, sb_ref, ta_ref, tb_ref, m_ref, acc_ref, *, seq):
    bf16, f32 = jnp.bfloat16, jnp.float32
    h = pl.program_id(1)
    qi = pl.program_id(2)
    nk = seq // TK_A
    n_steps = _steps_per_iteration(nk)
    c = c_ref[h]

    @pl.when(qi == 0)
    def _():
        lane = lax.broadcasted_iota(jnp.int32, (TK_A, LANES), 1)
        kb0 = kbias_ref[0, 0].astype(f32)
        kb1 = kbias_ref[0, 1].astype(f32)

        def body(j, carry):
            rows = pl.ds(pl.multiple_of(j * TK_A, TK_A), TK_A)
            kblk = k_ref[0, rows, :].astype(f32)
            kp_ref[0, rows, :] = jnp.where(lane < HEAD_DIM, kblk, kb0).astype(bf16)
            kp_ref[1, rows, :] = jnp.where(lane >= HEAD_DIM, kblk, kb1).astype(bf16)
            return carry

        lax.fori_loop(0, nk, body, 0)

    row = lax.broadcasted_iota(jnp.int32, (2 * HEAD_DIM, TQ_A), 0)
    qblk = qt_ref[0].astype(f32)
    for sub in range(2):
        own = (row < HEAD_DIM) if sub == 0 else (row >= HEAD_DIM)
        b0 = HEAD_DIM if sub == 0 else 0
        brow = (row >= b0) & (row < b0 + N_BIAS_LANES)
        base = jnp.where(own, qblk, 0.0)
        for var, sigma in enumerate((1.0, -1.0, 0.0)):
            qv_ref[3 * sub + var] = jnp.where(brow, sigma, base).astype(bf16)

    m_ref[...] = jnp.full(m_ref.shape, MASKED, f32)
    acc_ref[...] = jnp.zeros(acc_ref.shape, f32)

    i0 = qi * TQ_A
    jd = i0 // TK_A
    di = lax.broadcasted_iota(jnp.int32, (1, TQ_A), 1)

    def update(sub, t, tmax, rvec, vt):
        m_old = m_ref[sub]
        m_new = jnp.maximum(m_old, tmax + rvec)
        alpha = jnp.exp2(m_old - m_new)
        p = jnp.exp2(t - (m_new - rvec))
        acc_ref[sub] = alpha * acc_ref[sub] + jnp.dot(vt, p.astype(bf16), preferred_element_type=f32)
        m_ref[sub] = m_new

    def put_scores(s_ref, t_ref, sub, t):
        s_ref[sub] = t
        t_ref[sub] = jnp.max(t, axis=0, keepdims=True)

    def block_of(step):
        jj = step - 1
        j = jnp.where(step == 0, jd, jj + (jj >= jd).astype(jnp.int32))
        return j, j > jd

    def produce(bufs, step):
        s_ref, t_ref = bufs
        j, after = block_of(step)
        var = after.astype(jnp.int32)
        rows = pl.ds(pl.multiple_of(j * TK_A, TK_A), TK_A)
        for sub in range(2):
            put_scores(s_ref, t_ref, sub,
                       jnp.dot(kp_ref[sub, rows, :], qv_ref[3 * sub + var], preferred_element_type=f32))

    def consume(bufs, step):
        s_ref, t_ref = bufs
        j, after = block_of(step)
        sgn_c = jnp.where(j == jd, 0.0, jnp.where(after, c, -c)).astype(f32)
        rvec = sgn_c * (i0 - j * TK_A + di).astype(f32)
        vt = vt_ref[0, 0, j]
        for sub in range(2):
            update(sub, s_ref[sub], t_ref[sub], rvec, vt)

    buf_a = (sa_ref, ta_ref)
    buf_b = (sb_ref, tb_ref)

    rows_d = pl.ds(pl.multiple_of(jd * TK_A, TK_A), TK_A)
    for sub in range(2):
        put_scores(sa_ref, ta_ref, sub,
                   jnp.dot(kp_ref[sub, rows_d, :], qv_ref[3 * sub + 2], preferred_element_type=f32) + dbias_ref[0])

    def steps(s0, produce_last):
        for k in range(0, n_steps, 2):
            produce(buf_b, s0 + k + 1)
            consume(buf_a, s0 + k)
            if produce_last or k + 2 < n_steps:
                produce(buf_a, s0 + k + 2)
            consume(buf_b, s0 + k + 1)

    def body(i, carry):
        steps(n_steps * i, True)
        return carry

    lax.fori_loop(0, trips_ref[0], body, 0)
    steps(nk - n_steps, False)

    nf = 2 * HEAD_DIM
    o = (acc_ref[0, :nf, :] / acc_ref[0, nf:nf + 1, :]
         - lam_ref[0] * (acc_ref[1, :nf, :] / acc_ref[1, nf:nf + 1, :]))
    y = o * lax.rsqrt(jnp.mean(o * o, axis=0, keepdims=True) + EPS) * g2_ref[...]
    o_ref[0] = y.T.astype(o_ref.dtype)


def _diff_attn(cvec, lam, qat, ka, vat, kbias, dbias, g2):
    B, _, T = qat.shape
    nq = T // TQ_A
    nkb = T // TK_A
    smem = pl.BlockSpec(memory_space=pltpu.SMEM)
    trips = jnp.full((1,), nkb // _steps_per_iteration(nkb) - 1, jnp.int32)
    return pl.pallas_call(
        functools.partial(_diff_attn_kernel, seq=T),
        grid=(B, H_A, nq),
        in_specs=[
            smem, smem, smem,
            pl.BlockSpec((1, 2 * HEAD_DIM, TQ_A), lambda b, h, q: (b, h, q)),
            pl.BlockSpec((1, T, 2 * HEAD_DIM), lambda b, h, q: (b, 0, h)),
            pl.BlockSpec((1, 1, nkb, VT_ROWS, TK_A), lambda b, h, q: (b, h, 0, 0, 0)),
            pl.BlockSpec((1, 2, TK_A, LANES), lambda b, h, q: (h, 0, 0, 0)),
            pl.BlockSpec((1, TK_A, TQ_A), lambda b, h, q: (h, 0, 0)),
            pl.BlockSpec((2 * HEAD_DIM, 1), lambda b, h, q: (0, 0)),
        ],
        out_specs=pl.BlockSpec((1, TQ_A, 2 * HEAD_DIM), lambda b, h, q: (b, q, h)),
        out_shape=jax.ShapeDtypeStruct((B, T, W_A), jnp.bfloat16),
        scratch_shapes=[
            pltpu.VMEM((2, T, LANES), jnp.bfloat16),
            pltpu.VMEM((6, 2 * HEAD_DIM, TQ_A), jnp.bfloat16),
            pltpu.VMEM((2, TK_A, TQ_A), jnp.float32),
            pltpu.VMEM((2, TK_A, TQ_A), jnp.float32),
            pltpu.VMEM((2, 1, TQ_A), jnp.float32),
            pltpu.VMEM((2, 1, TQ_A), jnp.float32),
            pltpu.VMEM((2, 1, TQ_A), jnp.float32),
            pltpu.VMEM((2, VT_ROWS, TQ_A), jnp.float32),
        ],
        compiler_params=_params(dimension_semantics=("parallel", "parallel", "arbitrary")),
        name="diff_attn",
    )(cvec, lam, trips, qat, ka, vat, kbias, dbias, g2)


def _win_attn_kernel(sink_ref, qt_ref, k_ref, vt_ref, bias_ref, o_ref, *, seq, win, lead):
    bf16, f32 = jnp.bfloat16, jnp.float32
    pair = pl.program_id(0)
    step = pl.program_id(2)
    nq = seq // TQ_W
    nf = 2 * HEAD_DIM
    row = lax.broadcasted_iota(jnp.int32, (nf, TQ_W), 0)
    sinks = [sink_ref[2 * pair + hf] for hf in range(2)]

    jobs = [(n, hf) for n in range(QBLOCKS_W) for hf in range(2)]
    kws, vts, variants, qts = [], [], [], []
    for n in range(QBLOCKS_W):
        qi = step * QBLOCKS_W + n
        ws = pl.multiple_of(jnp.clip(qi * TQ_W - lead, 0, seq - win), LANES)
        variants.append(jnp.where(qi == 0, 0, jnp.where(qi == nq - 1, 2, 1)))
        kws.append(k_ref[0, pl.ds(ws, win), :])
        slab = ws // LANES
        vts.append(jnp.concatenate([vt_ref[0, 0, slab + c] for c in range(win // LANES)], axis=1))
        qts.append(qt_ref[0, :, n * TQ_W:(n + 1) * TQ_W].astype(f32))
    qms = [jnp.where((row < HEAD_DIM) if hf == 0 else (row >= HEAD_DIM), qts[n], 0.0).astype(bf16)
           for n, hf in jobs]
    sts = [jnp.dot(kws[n], qms[j], preferred_element_type=f32) + bias_ref[variants[n], hf]
           for j, (n, hf) in enumerate(jobs)]
    ms = [jnp.maximum(jnp.max(sts[j], axis=0, keepdims=True), sinks[hf]) for j, (n, hf) in enumerate(jobs)]
    pts = [jnp.exp2(sts[j] - ms[j]).astype(bf16) for j in range(len(jobs))]
    accs = [jnp.dot(vts[n], pts[j], preferred_element_type=f32) for j, (n, hf) in enumerate(jobs)]
    ots = [accs[j][hf * HEAD_DIM:(hf + 1) * HEAD_DIM] / (accs[j][nf:nf + 1] + jnp.exp2(sinks[hf] - ms[j]))
           for j, (n, hf) in enumerate(jobs)]
    for n in range(QBLOCKS_W):
        ot = jnp.concatenate([ots[2 * n], ots[2 * n + 1]], axis=0)
        o_ref[0, n * TQ_W:(n + 1) * TQ_W, :] = ot.T.astype(o_ref.dtype)


def _win_attn(sink2, qt, k, vt, bias, *, win, lead, kv_group):
    B, width, T = qt.shape
    n_pairs = width // LANES
    tq_step = QBLOCKS_W * TQ_W
    return pl.pallas_call(
        functools.partial(_win_attn_kernel, seq=T, win=win, lead=lead),
        grid=(n_pairs, B, T // tq_step),
        in_specs=[
            pl.BlockSpec(memory_space=pltpu.SMEM),
            pl.BlockSpec((1, LANES, tq_step), lambda p, b, s: (b, p, s)),
            pl.BlockSpec((1, T, LANES), lambda p, b, s: (b, 0, p // kv_group)),
            pl.BlockSpec((1, 1, T // LANES, VT_ROWS, LANES), lambda p, b, s: (b, p // kv_group, 0, 0, 0)),
            pl.BlockSpec((3, 2, win, TQ_W), lambda p, b, s: (0, p, 0, 0)),
        ],
        out_specs=pl.BlockSpec((1, tq_step, LANES), lambda p, b, s: (b, s, p)),
        out_shape=jax.ShapeDtypeStruct((B, T, width), jnp.bfloat16),
        compiler_params=_params(dimension_semantics=("parallel", "parallel", "arbitrary")),
        name="win_attn",
    )(sink2, qt, k, vt, bias)


def _merge_kernel(x_ref, oa_ref, ob_ref, oc_ref, za_ref, zb_ref, zc_ref, g_ref,
                  wa_ref, wb_ref, wc_ref, wo_ref, y_ref):
    bf16, f32 = jnp.bfloat16, jnp.float32

    def sigmoid(v):
        return 0.5 * jnp.tanh(0.5 * v) + 0.5

    def branch(o_ref, z_ref, w_ref, i):
        z = z_ref[0].astype(f32)
        u = o_ref[0].astype(f32) * (z * sigmoid(z))
        y = jnp.dot(u.astype(bf16), w_ref[...], preferred_element_type=f32)
        gate = g_ref[0, :, i * D_MODEL:(i + 1) * D_MODEL].astype(f32)
        return sigmoid(gate) * y

    m = branch(oa_ref, za_ref, wa_ref, 0) + branch(ob_ref, zb_ref, wb_ref, 1) + branch(oc_ref, zc_ref, wc_ref, 2)
    y_ref[0] = x_ref[0] + jnp.dot(m.astype(bf16), wo_ref[...], preferred_element_type=f32)


def _merge(x, oa, ob, oc, za, zb, zc, g, wa, wb, wc, wo):
    B, T, _ = x.shape
    tok = lambda w: pl.BlockSpec((1, TM_MERGE, w), lambda b, t: (b, t, 0))
    const2 = lambda a: _resident(a.shape, lambda b, t: (0, 0))
    return pl.pallas_call(
        _merge_kernel,
        grid=(B, T // TM_MERGE),
        in_specs=[tok(D_MODEL), tok(W_A), tok(W_B), tok(W_C), tok(W_A), tok(W_B), tok(W_C), tok(3 * D_MODEL),
                  const2(wa), const2(wb), const2(wc), const2(wo)],
        out_specs=tok(D_MODEL),
        out_shape=jax.ShapeDtypeStruct(x.shape, x.dtype),
        compiler_params=_params(dimension_semantics=("parallel", "parallel")),
        name="merge",
    )(x, oa, ob, oc, za, zb, zc, g, wa, wb, wc, wo)


def _split3_bf16(v):
    hi = v.astype(jnp.bfloat16)
    r1 = v - hi.astype(jnp.float32)
    mid = r1.astype(jnp.bfloat16)
    lo = (r1 - mid.astype(jnp.float32)).astype(jnp.bfloat16)
    return hi, mid, lo


def _diff_constants():
    c = np.asarray(_alibi_slopes(H_A), np.float32) * np.float32(LOG2E)
    col = jnp.asarray(c)[:, None] * jnp.arange(TK_A, dtype=jnp.float32)[None, :]
    pieces = jnp.stack(_split3_bf16(col), axis=-1)
    kbias = jnp.zeros((H_A, 2, TK_A, LANES), jnp.bfloat16)
    kbias = kbias.at[:, 0, :, HEAD_DIM:HEAD_DIM + N_BIAS_LANES].set(pieces)
    kbias = kbias.at[:, 1, :, 0:N_BIAS_LANES].set(pieces)
    dist = np.abs(np.arange(TQ_A)[None, :] - np.arange(TK_A)[:, None]).astype(np.float32)
    dbias = jnp.asarray(-c[:, None, None] * dist[None])
    return jnp.asarray(c), kbias, dbias


def _window_bias_c():
    slopes = np.asarray(_alibi_slopes(H_C), np.float64)
    di = np.arange(TQ_W)[None, :]
    dj = np.arange(WIN_C)[:, None]
    tiles = []
    for lead in (0, WINDOW, WIN_C - TQ_W):
        dist = np.abs(lead + di - dj)
        tile = np.where(dist <= WINDOW, -(slopes[:, None, None] * LOG2E) * dist[None], MASKED)
        tiles.append(tile)
    return jnp.asarray(np.stack(tiles).astype(np.float32))


def _window_bias_b(rpb):
    q_rows = TQ_W // GRID_W
    w_rows = WIN_B // GRID_W
    n_dr, n_dc = 2 * MAX_KH - 1, 2 * KW - 1
    table = jnp.pad(rpb.astype(jnp.float32) * LOG2E, ((0, 0), (0, 1), (0, 1)), constant_values=MASKED)
    qc = np.arange(GRID_W)[:, None]
    kc = np.arange(GRID_W)[None, :]
    cstart = np.clip(qc - KW // 2, 0, GRID_W - KW)
    col_ok = (kc >= cstart) & (kc < cstart + KW)
    dc = np.where(col_ok, kc - qc + (KW - 1), n_dc)
    col_sel = dc[:, :, None] == np.arange(n_dc + 1)
    qr = np.arange(q_rows)[:, None]
    wr = np.arange(w_rows)[None, :]
    row_sel = []
    for lead_rows, rs in ((0, 0 * qr), (MAX_KH // 2, qr), (w_rows - q_rows, 0 * qr + MAX_KH // 2)):
        row_ok = (wr >= rs) & (wr < rs + MAX_KH)
        dr = np.where(row_ok, wr - (lead_rows + qr) + (MAX_KH - 1), n_dr)
        row_sel.append(dr[:, :, None] == np.arange(n_dr + 1))
    row_sel = jnp.asarray(np.stack(row_sel), jnp.float32)
    vals = jnp.einsum("vrwd,hde,cke->vhwkrc", row_sel, table, jnp.asarray(col_sel, jnp.float32),
                      precision=lax.Precision.HIGHEST)
    return vals.reshape(3, H_B, WIN_B, TQ_W)


def _layer_weights(l, w_in, norm_g, qk_gain_a, qk_gain_b, qk_gain_c):
    bf16, f32 = jnp.bfloat16, jnp.float32
    w = w_in[l]
    sizes = (W_A,) * 4 + (W_B,) * 4 + (W_C, KV_C * HEAD_DIM, KV_C * HEAD_DIM, W_C) + (D_MODEL,) * 3
    offs = np.concatenate([[0], np.cumsum(sizes)])
    names = ("qa", "ka", "va", "za", "qb", "kb", "vb", "zb", "qc", "kc", "vc", "zc", "ga", "gb", "gc")
    cols = {n: w[:, offs[i]:offs[i + 1]] for i, n in enumerate(names)}
    dup = lambda a: jnp.concatenate([a[:, :HEAD_DIM], a[:, :HEAD_DIM], a[:, HEAD_DIM:], a[:, HEAD_DIM:]], axis=1)
    cols["kc"] = dup(cols["kc"])
    cols["vc"] = dup(cols["vc"])
    wtok = jnp.concatenate([cols[n] for n, _ in _TOK_SIZES], axis=1).astype(bf16)
    wfeat = jnp.concatenate([cols[n] for n, _ in _FEAT_SIZES], axis=1).T.astype(bf16)
    qscale = QK_SCALE * LOG2E
    tile8 = lambda g: jnp.tile(g.astype(f32), W_A // HEAD_DIM)
    zero = jnp.zeros((W_A,), f32)
    gains = jnp.stack([tile8(qk_gain_a[l, 1]), tile8(qk_gain_b[l, 1]), tile8(qk_gain_c[l, 1])] + [zero] * 5)
    gfeat = jnp.stack([qk_gain_a[l, 0], qk_gain_b[l, 0], qk_gain_c[l, 0]]).astype(f32)[:, :, None] * qscale
    ng = norm_g[l].astype(f32)[None, :]
    return ng, wtok, wfeat, gains, gfeat


def _trunk(x, layers, consts):
    cvec, kbias, dbias, gmat, bias_c, no_sink = consts
    for lw in layers:
        (ng, wtok, wfeat, gains, gfeat, lam, g2, bias_b, sink2, wa, wb, wc, wo) = lw
        (qat, ka, vat, za, qbt, kb, vbt, zb, qct, kc, vct, zc, g) = _inproj(x, ng, wtok, wfeat, gmat, gains, gfeat)
        oa = _diff_attn(cvec, lam, qat, ka, vat, kbias, dbias, g2)
        ob = _win_attn(no_sink, qbt, kb, vbt, bias_b, win=WIN_B, lead=(MAX_KH // 2) * GRID_W, kv_group=1)
        oc = _win_attn(sink2, qct, kc, vct, bias_c, win=WIN_C, lead=WINDOW, kv_group=2)
        x = _merge(x, oa, ob, oc, za, zb, zc, g, wa, wb, wc, wo)
    return x


def kernel(x_prompt, x_sample, norm_g, w_in, qk_gain_a, lambda_a, subln_g_a, qk_gain_b, rpb_b,
           qk_gain_c, sink_c, w_proj_a, w_proj_b, w_proj_c, w_out):
    bf16, f32 = jnp.bfloat16, jnp.float32
    for x in (x_prompt, x_sample):
        T = x.shape[1]
        nk = T // TK_A
        assert T % TK_A == 0 and nk % _steps_per_iteration(nk) == 0
        assert T % (QBLOCKS_W * TQ_W) == 0 and T >= 3 * TQ_W and T >= WIN_B and x.shape[2] == D_MODEL
    cvec, kbias, dbias = _diff_constants()
    gidx = np.arange(NORM_TILE) // HEAD_DIM
    gmat = jnp.asarray((gidx[:, None] == gidx[None, :]).astype(np.float32) / HEAD_DIM, bf16)
    consts = (cvec, kbias, dbias, gmat, _window_bias_c(), jnp.full((H_B,), MASKED, f32))
    layers = []
    for l in range(DEPTH):
        lam_init = 0.8 - 0.6 * math.exp(-0.3 * l)
        lv = lambda_a[l].astype(f32)
        lam = (jnp.exp(jnp.sum(lv[0] * lv[1])) - jnp.exp(jnp.sum(lv[2] * lv[3])) + lam_init).reshape(1)
        g2 = (subln_g_a[l].astype(f32) * (1.0 - lam_init))[:, None]
        layers.append(_layer_weights(l, w_in, norm_g, qk_gain_a, qk_gain_b, qk_gain_c)
                      + (lam, g2, _window_bias_b(rpb_b[l]), sink_c[l].astype(f32) * LOG2E,
                         w_proj_a[l].astype(bf16), w_proj_b[l].astype(bf16), w_proj_c[l].astype(bf16),
                         w_out[l].astype(bf16)))
    return (_trunk(x_prompt, layers, consts), _trunk(x_sample, layers, consts))
```

```python
import functools
import math

import numpy as np
import jax
import jax.numpy as jnp
from jax import lax
from jax.experimental import pallas as pl
from jax.experimental.pallas import tpu as pltpu

D_MODEL = 1024
DEPTH = 2
HEAD_DIM = 64
EPS = 1e-6
H_A = 4
W_A = H_A * 2 * HEAD_DIM
H_B = 8
W_B = H_B * HEAD_DIM
GRID_W = 64
MAX_KH = 8
KW = 16
H_C = 8
KV_C = 2
W_C = H_C * HEAD_DIM
WINDOW = 128

LOG2E = math.log2(math.e)
QK_SCALE = HEAD_DIM ** -0.5
MASKED = -1e30

LANES = 128
TM = 512
TM_MERGE = 1024
NORM_TILE = 256
TQ_A = TM
TK_A = TM
STEPS_A = 16
MIN_STEPS_A = 8
TQ_W = 256
QBLOCKS_W = 16
WIN_B = (MAX_KH + 4) * GRID_W
WIN_C = TQ_W + 2 * WINDOW
N_BIAS_LANES = 3
BF16_ROWS = 16
VT_ROWS = 2 * HEAD_DIM + BF16_ROWS
VMEM_LIMIT_BYTES = 56 * 1024 * 1024


def _offsets(sizes):
    offs, o = {}, 0
    for name, size in sizes:
        offs[name] = (o, o + size)
        o += size
    return offs


_TOK_SIZES = (("ka", W_A), ("za", W_A), ("kb", W_B), ("zb", W_B), ("kc", 4 * HEAD_DIM), ("zc", W_C),
              ("ga", D_MODEL), ("gb", D_MODEL), ("gc", D_MODEL))
_FEAT_SIZES = (("qa", W_A), ("va", W_A), ("qb", W_B), ("vb", W_B), ("qc", W_C), ("vc", 4 * HEAD_DIM))
_TOK_OFF = _offsets(_TOK_SIZES)
_FEAT_OFF = _offsets(_FEAT_SIZES)


def _alibi_slopes(n):
    return [2.0 ** (-8.0 * (i + 1) / n) for i in range(n)]


def _params(**kw):
    return pltpu.CompilerParams(vmem_limit_bytes=VMEM_LIMIT_BYTES, **kw)


def _resident(block_shape, index_map):
    return pl.BlockSpec(block_shape, index_map, pipeline_mode=pl.Buffered(1))


def _inproj_kernel(x_ref, ng_ref, wtok_ref, wfeat_ref, gmat_ref, gains_ref, gfeat_ref,
                   qat_ref, ka_ref, vat_ref, za_ref, qbt_ref, kb_ref, vbt_ref, zb_ref,
                   qct_ref, kc_ref, vct_ref, zc_ref, g_ref):
    bf16, f32 = jnp.bfloat16, jnp.float32
    x = x_ref[0]
    ms = jnp.mean(x * x, axis=-1, keepdims=True)
    h = (x * lax.rsqrt(ms + EPS) * ng_ref[...]).astype(bf16)

    def proj(name):
        c0, c1 = _TOK_OFF[name]
        return jnp.dot(h, wtok_ref[:, c0:c1], preferred_element_type=f32)

    def normed(name, gain_row):
        p = proj(name)
        w = p.shape[-1]
        pp = (p * p).astype(bf16)
        ss = jnp.concatenate([jnp.dot(pp[:, c0:c0 + NORM_TILE], gmat_ref[...], preferred_element_type=f32)
                              for c0 in range(0, w, NORM_TILE)], axis=1)
        return (p * lax.rsqrt(ss + EPS) * gains_ref[gain_row:gain_row + 1, :w]).astype(bf16)

    ka_ref[0] = normed("ka", 0)
    za_ref[0] = proj("za").astype(bf16)
    kb_ref[0] = normed("kb", 1)
    zb_ref[0] = proj("zb").astype(bf16)
    kc_ref[0] = normed("kc", 2)
    zc_ref[0] = proj("zc").astype(bf16)
    for i, name in enumerate(("ga", "gb", "gc")):
        g_ref[0, :, i * D_MODEL:(i + 1) * D_MODEL] = proj(name).astype(bf16)

    def proj_t(name):
        r0, r1 = _FEAT_OFF[name]
        return lax.dot_general(wfeat_ref[r0:r1, :], h, (((1,), (1,)), ((), ())), preferred_element_type=f32)

    def put_q(name, gain_idx, out_ref):
        pf = proj_t(name)
        for g in range(pf.shape[0] // HEAD_DIM):
            blk = pf[g * HEAD_DIM:(g + 1) * HEAD_DIM]
            r = lax.rsqrt(jnp.mean(blk * blk, axis=0, keepdims=True) + EPS)
            out_ref[0, g * HEAD_DIM:(g + 1) * HEAD_DIM, :] = (blk * r * gfeat_ref[gain_idx]).astype(bf16)

    put_q("qa", 0, qat_ref)
    put_q("qb", 1, qbt_ref)
    put_q("qc", 2, qct_ref)

    pad_row = lax.broadcasted_iota(jnp.int32, (BF16_ROWS, TM), 0)
    ones_row = jnp.where(pad_row == 0, 1.0, 0.0).astype(bf16)
    nf = 2 * HEAD_DIM
    pf = proj_t("va")
    for hh in range(H_A):
        vat_ref[0, hh, 0, :nf, :] = pf[hh * nf:(hh + 1) * nf].astype(bf16)
        vat_ref[0, hh, 0, nf:, :] = ones_row
    for name, out_ref in (("vb", vbt_ref), ("vc", vct_ref)):
        pf = proj_t(name)
        for p in range(pf.shape[0] // nf):
            for s in range(TM // LANES):
                out_ref[0, p, s, :nf, :] = pf[p * nf:(p + 1) * nf, s * LANES:(s + 1) * LANES].astype(bf16)
                out_ref[0, p, s, nf:, :] = ones_row[:, :LANES]


def _inproj(x, ng, wtok, wfeat, gmat, gains, gfeat):
    B, T, _ = x.shape
    nt = T // TM
    bf16 = jnp.bfloat16
    tok = lambda w: pl.BlockSpec((1, TM, w), lambda b, t: (b, t, 0))
    feat = lambda w: pl.BlockSpec((1, w, TM), lambda b, t: (b, 0, t))
    slabs = TM // LANES
    vt_slabs = lambda n: pl.BlockSpec((1, n, slabs, VT_ROWS, LANES), lambda b, t: (b, 0, t, 0, 0))
    const = lambda a: _resident(a.shape, lambda b, t: (0,) * a.ndim)
    out_shape = (
        jax.ShapeDtypeStruct((B, W_A, T), bf16),
        jax.ShapeDtypeStruct((B, T, W_A), bf16),
        jax.ShapeDtypeStruct((B, H_A, nt, VT_ROWS, TM), bf16),
        jax.ShapeDtypeStruct((B, T, W_A), bf16),
        jax.ShapeDtypeStruct((B, W_B, T), bf16),
        jax.ShapeDtypeStruct((B, T, W_B), bf16),
        jax.ShapeDtypeStruct((B, H_B // 2, T // LANES, VT_ROWS, LANES), bf16),
        jax.ShapeDtypeStruct((B, T, W_B), bf16),
        jax.ShapeDtypeStruct((B, W_C, T), bf16),
        jax.ShapeDtypeStruct((B, T, 4 * HEAD_DIM), bf16),
        jax.ShapeDtypeStruct((B, KV_C, T // LANES, VT_ROWS, LANES), bf16),
        jax.ShapeDtypeStruct((B, T, W_C), bf16),
        jax.ShapeDtypeStruct((B, T, 3 * D_MODEL), bf16),
    )
    out_specs = (
        feat(W_A),
        tok(W_A),
        pl.BlockSpec((1, H_A, 1, VT_ROWS, TM), lambda b, t: (b, 0, t, 0, 0)),
        tok(W_A), feat(W_B), tok(W_B), vt_slabs(H_B // 2), tok(W_B), feat(W_C),
        tok(4 * HEAD_DIM), vt_slabs(KV_C), tok(W_C), tok(3 * D_MODEL),
    )
    return pl.pallas_call(
        _inproj_kernel,
        grid=(B, nt),
        in_specs=[pl.BlockSpec((1, TM, D_MODEL), lambda b, t: (b, t, 0)),
                  const(ng), const(wtok), const(wfeat), const(gmat), const(gains), const(gfeat)],
        out_specs=out_specs,
        out_shape=out_shape,
        compiler_params=_params(dimension_semantics=("parallel", "parallel")),
        name="inproj",
    )(x, ng, wtok, wfeat, gmat, gains, gfeat)


def _steps_per_iteration(nk):
    steps = STEPS_A
    while steps > MIN_STEPS_A and nk < 2 * steps:
        steps //= 2
    return steps


def _diff_attn_kernel(c_ref, lam_ref, trips_ref, qt_ref, k_ref, vt_ref, kbias_ref, dbias_ref, g2_ref, o_ref,
                      kp_ref, qv_ref, sa_ref, sb_ref, ta_ref, tb_ref, m_ref, acc_ref, *, seq):
    bf16, f32 = jnp.bfloat16, jnp.float32
    h = pl.program_id(1)
    qi = pl.program_id(2)
    nk = seq // TK_A
    n_steps = _steps_per_iteration(nk)
    c = c_ref[h]

    @pl.when(qi == 0)
    def _():
        lane = lax.broadcasted_iota(jnp.int32, (TK_A, LANES), 1)
        kb0 = kbias_ref[0, 0].astype(f32)
        kb1 = kbias_ref[0, 1].astype(f32)

        def body(j, carry):
            rows = pl.ds(pl.multiple_of(j * TK_A, TK_A), TK_A)
            kblk = k_ref[0, rows, :].astype(f32)
            kp_ref[0, rows, :] = jnp.where(lane < HEAD_DIM, kblk, kb0).astype(bf16)
            kp_ref[1, rows, :] = jnp.where(lane >= HEAD_DIM, kblk, kb1).astype(bf16)
            return carry

        lax.fori_loop(0, nk, body, 0)

    row = lax.broadcasted_iota(jnp.int32, (2 * HEAD_DIM, TQ_A), 0)
    qblk = qt_ref[0].astype(f32)
    for sub in range(2):
        own = (row < HEAD_DIM) if sub == 0 else (row >= HEAD_DIM)
        b0 = HEAD_DIM if sub == 0 else 0
        brow = (row >= b0) & (row < b0 + N_BIAS_LANES)
        base = jnp.where(own, qblk, 0.0)
        for var, sigma in enumerate((1.0, -1.0, 0.0)):
            qv_ref[3 * sub + var] = jnp.where(brow, sigma, base).astype(bf16)

    m_ref[...] = jnp.full(m_ref.shape, MASKED, f32)
    acc_ref[...] = jnp.zeros(acc_ref.shape, f32)

    i0 = qi * TQ_A
    jd = i0 // TK_A
    di = lax.broadcasted_iota(jnp.int32, (1, TQ_A), 1)

    def update(sub, t, tmax, rvec, vt):
        m_old = m_ref[sub]
        m_new = jnp.maximum(m_old, tmax + rvec)
        alpha = jnp.exp2(m_old - m_new)
        p = jnp.exp2(t - (m_new - rvec))
        acc_ref[sub] = alpha * acc_ref[sub] + jnp.dot(vt, p.astype(bf16), preferred_element_type=f32)
        m_ref[sub] = m_new

    def put_scores(s_ref, t_ref, sub, t):
        s_ref[sub] = t
        t_ref[sub] = jnp.max(t, axis=0, keepdims=True)

    def block_of(step):
        jj = step - 1
        j = jnp.where(step == 0, jd, jj + (jj >= jd).astype(jnp.int32))
        return j, j > jd

    def produce(bufs, step):
        s_ref, t_ref = bufs
        j, after = block_of(step)
        var = after.astype(jnp.int32)
        rows = pl.ds(pl.multiple_of(j * TK_A, TK_A), TK_A)
        for sub in range(2):
            put_scores(s_ref, t_ref, sub,
                       jnp.dot(kp_ref[sub, rows, :], qv_ref[3 * sub + var], preferred_element_type=f32))

    def consume(bufs, step):
        s_ref, t_ref = bufs
        j, after = block_of(step)
        sgn_c = jnp.where(j == jd, 0.0, jnp.where(after, c, -c)).astype(f32)
        rvec = sgn_c * (i0 - j * TK_A + di).astype(f32)
        vt = vt_ref[0, 0, j]
        for sub in range(2):
            update(sub, s_ref[sub], t_ref[sub], rvec, vt)

    buf_a = (sa_ref, ta_ref)
    buf_b = (sb_ref, tb_ref)

    rows_d = pl.ds(pl.multiple_of(jd * TK_A, TK_A), TK_A)
    for sub in range(2):
        put_scores(sa_ref, ta_ref, sub,
                   jnp.dot(kp_ref[sub, rows_d, :], qv_ref[3 * sub + 2], preferred_element_type=f32) + dbias_ref[0])

    def steps(s0, produce_last):
        for k in range(0, n_steps, 2):
            produce(buf_b, s0 + k + 1)
            consume(buf_a, s0 + k)
            if produce_last or k + 2 < n_steps:
                produce(buf_a, s0 + k + 2)
            consume(buf_b, s0 + k + 1)

    def body(i, carry):
        steps(n_steps * i, True)
        return carry

    lax.fori_loop(0, trips_ref[0], body, 0)
    steps(nk - n_steps, False)

    nf = 2 * HEAD_DIM
    o = (acc_ref[0, :nf, :] / acc_ref[0, nf:nf + 1, :]
         - lam_ref[0] * (acc_ref[1, :nf, :] / acc_ref[1, nf:nf + 1, :]))
    y = o * lax.rsqrt(jnp.mean(o * o, axis=0, keepdims=True) + EPS) * g2_ref[...]
    o_ref[0] = y.T.astype(o_ref.dtype)


def _diff_attn(cvec, lam, qat, ka, vat, kbias, dbias, g2):
    B, _, T = qat.shape
    nq = T // TQ_A
    nkb = T // TK_A
    smem = pl.BlockSpec(memory_space=pltpu.SMEM)
    trips = jnp.full((1,), nkb // _steps_per_iteration(nkb) - 1, jnp.int32)
    return pl.pallas_call(
        functools.partial(_diff_attn_kernel, seq=T),
        grid=(B, H_A, nq),
        in_specs=[
            smem, smem, smem,
            pl.BlockSpec((1, 2 * HEAD_DIM, TQ_A), lambda b, h, q: (b, h, q)),
            pl.BlockSpec((1, T, 2 * HEAD_DIM), lambda b, h, q: (b, 0, h)),
            pl.BlockSpec((1, 1, nkb, VT_ROWS, TK_A), lambda b, h, q: (b, h, 0, 0, 0)),
            pl.BlockSpec((1, 2, TK_A, LANES), lambda b, h, q: (h, 0, 0, 0)),
            pl.BlockSpec((1, TK_A, TQ_A), lambda b, h, q: (h, 0, 0)),
            pl.BlockSpec((2 * HEAD_DIM, 1), lambda b, h, q: (0, 0)),
        ],
        out_specs=pl.BlockSpec((1, TQ_A, 2 * HEAD_DIM), lambda b, h, q: (b, q, h)),
        out_shape=jax.ShapeDtypeStruct((B, T, W_A), jnp.bfloat16),
        scratch_shapes=[
            pltpu.VMEM((2, T, LANES), jnp.bfloat16),
            pltpu.VMEM((6, 2 * HEAD_DIM, TQ_A), jnp.bfloat16),
            pltpu.VMEM((2, TK_A, TQ_A), jnp.float32),
            pltpu.VMEM((2, TK_A, TQ_A), jnp.float32),
            pltpu.VMEM((2, 1, TQ_A), jnp.float32),
            pltpu.VMEM((2, 1, TQ_A), jnp.float32),
            pltpu.VMEM((2, 1, TQ_A), jnp.float32),
            pltpu.VMEM((2, VT_ROWS, TQ_A), jnp.float32),
        ],
        compiler_params=_params(dimension_semantics=("parallel", "parallel", "arbitrary")),
        name="diff_attn",
    )(cvec, lam, trips, qat, ka, vat, kbias, dbias, g2)


def _win_attn_kernel(sink_ref, qt_ref, k_ref, vt_ref, bias_ref, o_ref, *, seq, win, lead):
    bf16, f32 = jnp.bfloat16, jnp.float32
    pair = pl.program_id(0)
    step = pl.program_id(2)
    nq = seq // TQ_W
    nf = 2 * HEAD_DIM
    row = lax.broadcasted_iota(jnp.int32, (nf, TQ_W), 0)
    sinks = [sink_ref[2 * pair + hf] for hf in range(2)]

    jobs = [(n, hf) for n in range(QBLOCKS_W) for hf in range(2)]
    kws, vts, variants, qts = [], [], [], []
    for n in range(QBLOCKS_W):
        qi = step * QBLOCKS_W + n
        ws = pl.multiple_of(jnp.clip(qi * TQ_W - lead, 0, seq - win), LANES)
        variants.append(jnp.where(qi == 0, 0, jnp.where(qi == nq - 1, 2, 1)))
        kws.append(k_ref[0, pl.ds(ws, win), :])
        slab = ws // LANES
        vts.append(jnp.concatenate([vt_ref[0, 0, slab + c] for c in range(win // LANES)], axis=1))
        qts.append(qt_ref[0, :, n * TQ_W:(n + 1) * TQ_W].astype(f32))
    qms = [jnp.where((row < HEAD_DIM) if hf == 0 else (row >= HEAD_DIM), qts[n], 0.0).astype(bf16)
           for n, hf in jobs]
    sts = [jnp.dot(kws[n], qms[j], preferred_element_type=f32) + bias_ref[variants[n], hf]
           for j, (n, hf) in enumerate(jobs)]
    ms = [jnp.maximum(jnp.max(sts[j], axis=0, keepdims=True), sinks[hf]) for j, (n, hf) in enumerate(jobs)]
    pts = [jnp.exp2((sts[j] - ms[j]).astype(bf16)) for j in range(len(jobs))]
    accs = [jnp.dot(vts[n], pts[j], preferred_element_type=f32) for j, (n, hf) in enumerate(jobs)]
    ots = [accs[j][hf * HEAD_DIM:(hf + 1) * HEAD_DIM] / (accs[j][nf:nf + 1] + jnp.exp2(sinks[hf] - ms[j]))
           for j, (n, hf) in enumerate(jobs)]
    for n in range(QBLOCKS_W):
        ot = jnp.concatenate([ots[2 * n], ots[2 * n + 1]], axis=0)
        o_ref[0, n * TQ_W:(n + 1) * TQ_W, :] = ot.T.astype(o_ref.dtype)


def _win_attn(sink2, qt, k, vt, bias, *, win, lead, kv_group):
    B, width, T = qt.shape
    n_pairs = width // LANES
    tq_step = QBLOCKS_W * TQ_W
    return pl.pallas_call(
        functools.partial(_win_attn_kernel, seq=T, win=win, lead=lead),
        grid=(n_pairs, B, T // tq_step),
        in_specs=[
            pl.BlockSpec(memory_space=pltpu.SMEM),
            pl.BlockSpec((1, LANES, tq_step), lambda p, b, s: (b, p, s)),
            pl.BlockSpec((1, T, LANES), lambda p, b, s: (b, 0, p // kv_group)),
            pl.BlockSpec((1, 1, T // LANES, VT_ROWS, LANES), lambda p, b, s: (b, p // kv_group, 0, 0, 0)),
            pl.BlockSpec((3, 2, win, TQ_W), lambda p, b, s: (0, p, 0, 0)),
        ],
        out_specs=pl.BlockSpec((1, tq_step, LANES), lambda p, b, s: (b, s, p)),
        out_shape=jax.ShapeDtypeStruct((B, T, width), jnp.bfloat16),
        compiler_params=_params(dimension_semantics=("parallel", "parallel", "arbitrary")),
        name="win_attn",
    )(sink2, qt, k, vt, bias)


def _merge_kernel(x_ref, oa_ref, ob_ref, oc_ref, za_ref, zb_ref, zc_ref, g_ref,
                  wa_ref, wb_ref, wc_ref, wo_ref, y_ref):
    bf16, f32 = jnp.bfloat16, jnp.float32

    def sigmoid(v):
        return 0.5 * jnp.tanh(0.5 * v) + 0.5

    def branch(o_ref, z_ref, w_ref, i):
        z = z_ref[0].astype(f32)
        u = o_ref[0].astype(f32) * (z * sigmoid(z))
        y = jnp.dot(u.astype(bf16), w_ref[...], preferred_element_type=f32)
        gate = g_ref[0, :, i * D_MODEL:(i + 1) * D_MODEL].astype(f32)
        return sigmoid(gate) * y

    m = branch(oa_ref, za_ref, wa_ref, 0) + branch(ob_ref, zb_ref, wb_ref, 1) + branch(oc_ref, zc_ref, wc_ref, 2)
    y_ref[0] = x_ref[0] + jnp.dot(m.astype(bf16), wo_ref[...], preferred_element_type=f32)


def _merge(x, oa, ob, oc, za, zb, zc, g, wa, wb, wc, wo):
    B, T, _ = x.shape
    tok = lambda w: pl.BlockSpec((1, TM_MERGE, w), lambda b, t: (b, t, 0))
    const2 = lambda a: _resident(a.shape, lambda b, t: (0, 0))
    return pl.pallas_call(
        _merge_kernel,
        grid=(B, T // TM_MERGE),
        in_specs=[tok(D_MODEL), tok(W_A), tok(W_B), tok(W_C), tok(W_A), tok(W_B), tok(W_C), tok(3 * D_MODEL),
                  const2(wa), const2(wb), const2(wc), const2(wo)],
        out_specs=tok(D_MODEL),
        out_shape=jax.ShapeDtypeStruct(x.shape, x.dtype),
        compiler_params=_params(dimension_semantics=("parallel", "parallel")),
        name="merge",
    )(x, oa, ob, oc, za, zb, zc, g, wa, wb, wc, wo)


def _split3_bf16(v):
    hi = v.astype(jnp.bfloat16)
    r1 = v - hi.astype(jnp.float32)
    mid = r1.astype(jnp.bfloat16)
    lo = (r1 - mid.astype(jnp.float32)).astype(jnp.bfloat16)
    return hi, mid, lo


def _diff_constants():
    c = np.asarray(_alibi_slopes(H_A), np.float32) * np.float32(LOG2E)
    col = jnp.asarray(c)[:, None] * jnp.arange(TK_A, dtype=jnp.float32)[None, :]
    pieces = jnp.stack(_split3_bf16(col), axis=-1)
    kbias = jnp.zeros((H_A, 2, TK_A, LANES), jnp.bfloat16)
    kbias = kbias.at[:, 0, :, HEAD_DIM:HEAD_DIM + N_BIAS_LANES].set(pieces)
    kbias = kbias.at[:, 1, :, 0:N_BIAS_LANES].set(pieces)
    dist = np.abs(np.arange(TQ_A)[None, :] - np.arange(TK_A)[:, None]).astype(np.float32)
    dbias = jnp.asarray(-c[:, None, None] * dist[None])
    return jnp.asarray(c), kbias, dbias


def _window_bias_c():
    slopes = np.asarray(_alibi_slopes(H_C), np.float64)
    di = np.arange(TQ_W)[None, :]
    dj = np.arange(WIN_C)[:, None]
    tiles = []
    for lead in (0, WINDOW, WIN_C - TQ_W):
        dist = np.abs(lead + di - dj)
        tile = np.where(dist <= WINDOW, -(slopes[:, None, None] * LOG2E) * dist[None], MASKED)
        tiles.append(tile)
    return jnp.asarray(np.stack(tiles).astype(np.float32))


def _window_bias_b(rpb):
    q_rows = TQ_W // GRID_W
    w_rows = WIN_B // GRID_W
    n_dr, n_dc = 2 * MAX_KH - 1, 2 * KW - 1
    table = jnp.pad(rpb.astype(jnp.float32) * LOG2E, ((0, 0), (0, 1), (0, 1)), constant_values=MASKED)
    qc = np.arange(GRID_W)[:, None]
    kc = np.arange(GRID_W)[None, :]
    cstart = np.clip(qc - KW // 2, 0, GRID_W - KW)
    col_ok = (kc >= cstart) & (kc < cstart + KW)
    dc = np.where(col_ok, kc - qc + (KW - 1), n_dc)
    col_sel = dc[:, :, None] == np.arange(n_dc + 1)
    qr = np.arange(q_rows)[:, None]
    wr = np.arange(w_rows)[None, :]
    row_sel = []
    for lead_rows, rs in ((0, 0 * qr), (MAX_KH // 2, qr), (w_rows - q_rows, 0 * qr + MAX_KH // 2)):
        row_ok = (wr >= rs) & (wr < rs + MAX_KH)
        dr = np.where(row_ok, wr - (lead_rows + qr) + (MAX_KH - 1), n_dr)
        row_sel.append(dr[:, :, None] == np.arange(n_dr + 1))
    row_sel = jnp.asarray(np.stack(row_sel), jnp.float32)
    vals = jnp.einsum("vrwd,hde,cke->vhwkrc", row_sel, table, jnp.asarray(col_sel, jnp.float32),
                      precision=lax.Precision.HIGHEST)
    return vals.reshape(3, H_B, WIN_B, TQ_W)


def _layer_weights(l, w_in, norm_g, qk_gain_a, qk_gain_b, qk_gain_c):
    bf16, f32 = jnp.bfloat16, jnp.float32
    w = w_in[l]
    sizes = (W_A,) * 4 + (W_B,) * 4 + (W_C, KV_C * HEAD_DIM, KV_C * HEAD_DIM, W_C) + (D_MODEL,) * 3
    offs = np.concatenate([[0], np.cumsum(sizes)])
    names = ("qa", "ka", "va", "za", "qb", "kb", "vb", "zb", "qc", "kc", "vc", "zc", "ga", "gb", "gc")
    cols = {n: w[:, offs[i]:offs[i + 1]] for i, n in enumerate(names)}
    dup = lambda a: jnp.concatenate([a[:, :HEAD_DIM], a[:, :HEAD_DIM], a[:, HEAD_DIM:], a[:, HEAD_DIM:]], axis=1)
    cols["kc"] = dup(cols["kc"])
    cols["vc"] = dup(cols["vc"])
    wtok = jnp.concatenate([cols[n] for n, _ in _TOK_SIZES], axis=1).astype(bf16)
    wfeat = jnp.concatenate([cols[n] for n, _ in _FEAT_SIZES], axis=1).T.astype(bf16)
    qscale = QK_SCALE * LOG2E
    tile8 = lambda g: jnp.tile(g.astype(f32), W_A // HEAD_DIM)
    zero = jnp.zeros((W_A,), f32)
    gains = jnp.stack([tile8(qk_gain_a[l, 1]), tile8(qk_gain_b[l, 1]), tile8(qk_gain_c[l, 1])] + [zero] * 5)
    gfeat = jnp.stack([qk_gain_a[l, 0], qk_gain_b[l, 0], qk_gain_c[l, 0]]).astype(f32)[:, :, None] * qscale
    ng = norm_g[l].astype(f32)[None, :]
    return ng, wtok, wfeat, gains, gfeat


def _trunk(x, layers, consts):
    cvec, kbias, dbias, gmat, bias_c, no_sink = consts
    for lw in layers:
        (ng, wtok, wfeat, gains, gfeat, lam, g2, bias_b, sink2, wa, wb, wc, wo) = lw
        (qat, ka, vat, za, qbt, kb, vbt, zb, qct, kc, vct, zc, g) = _inproj(x, ng, wtok, wfeat, gmat, gains, gfeat)
        oa = _diff_attn(cvec, lam, qat, ka, vat, kbias, dbias, g2)
        ob = _win_attn(no_sink, qbt, kb, vbt, bias_b, win=WIN_B, lead=(MAX_KH // 2) * GRID_W, kv_group=1)
        oc = _win_attn(sink2, qct, kc, vct, bias_c, win=WIN_C, lead=WINDOW, kv_group=2)
        x = _merge(x, oa, ob, oc, za, zb, zc, g, wa, wb, wc, wo)
    return x


def kernel(x_prompt, x_sample, norm_g, w_in, qk_gain_a, lambda_a, subln_g_a, qk_gain_b, rpb_b,
           qk_gain_c, sink_c, w_proj_a, w_proj_b, w_proj_c, w_out):
    bf16, f32 = jnp.bfloat16, jnp.float32
    for x in (x_prompt, x_sample):
        T = x.shape[1]
        nk = T // TK_A
        assert T % TK_A == 0 and nk % _steps_per_iteration(nk) == 0
        assert T % (QBLOCKS_W * TQ_W) == 0 and T >= 3 * TQ_W and T >= WIN_B and x.shape[2] == D_MODEL
    cvec, kbias, dbias = _diff_constants()
    gidx = np.arange(NORM_TILE) // HEAD_DIM
    gmat = jnp.asarray((gidx[:, None] == gidx[None, :]).astype(np.float32) / HEAD_DIM, bf16)
    consts = (cvec, kbias, dbias, gmat, _window_bias_c(), jnp.full((H_B,), MASKED, f32))
    layers = []
    for l in range(DEPTH):
        lam_init = 0.8 - 0.6 * math.exp(-0.3 * l)
        lv = lambda_a[l].astype(f32)
        lam = (jnp.exp(jnp.sum(lv[0] * lv[1])) - jnp.exp(jnp.sum(lv[2] * lv[3])) + lam_init).reshape(1)
        g2 = (subln_g_a[l].astype(f32) * (1.0 - lam_init))[:, None]
        layers.append(_layer_weights(l, w_in, norm_g, qk_gain_a, qk_gain_b, qk_gain_c)
                      + (lam, g2, _window_bias_b(rpb_b[l]), sink_c[l].astype(f32) * LOG2E,
                         w_proj_a[l].astype(bf16), w_proj_b[l].astype(bf16), w_proj_c[l].astype(bf16),
                         w_out[l].astype(bf16)))
    return (_trunk(x_prompt, layers, consts), _trunk(x_sample, layers, consts))
```
